```python
import math
import jax
import jax.numpy as jnp
from jax import lax
import numpy as np

D_MODEL = 1024
BATCH = 8
SEQ = 2048
DEPTH = 4
DEC_BATCH = 32
DEC_SEQ = 1
PAST_LEN = 8192
PAGE_SIZE = 128

N_META = 16
HEAD_DIM = 64
MIX_WIDTH = D_MODEL // 2
FOX_HEADS = MIX_WIDTH // HEAD_DIM
RWKV_HEADS = MIX_WIDTH // HEAD_DIM
RWKV_N = HEAD_DIM
RWKV_W_LORA = max(32, int(round(1.8 * D_MODEL ** 0.5 / 32)) * 32)
RWKV_A_LORA = max(32, int(round(1.8 * D_MODEL ** 0.5 / 32)) * 32)
RWKV_G_LORA = max(32, int(round(0.6 * D_MODEL ** 0.8 / 32)) * 32)
RW_SIZES = (MIX_WIDTH, MIX_WIDTH, MIX_WIDTH, RWKV_W_LORA, RWKV_A_LORA, RWKV_G_LORA)
RW_COLS = sum(RW_SIZES)
S5_GROUP = 16
S5_GROUPS = MIX_WIDTH // S5_GROUP
S5_STATE = 64
N_BRANCH = 3
IN_SIZES = (MIX_WIDTH, MIX_WIDTH, MIX_WIDTH, FOX_HEADS, RW_COLS, MIX_WIDTH, N_BRANCH * D_MODEL)
IN_COLS = sum(IN_SIZES)
D_FF = ((8 * D_MODEL // 3 + 255) // 256) * 256
Q_BLOCK = 128
NORM_EPS = 1e-6
LNX_EPS = 64e-5
POOL_NUM = 5
POOL_DEN = 4

kernel_name = 'hybrid_fox_rwkv7_s5_macaron_step'


def _split(x, sizes):
    idx = np.cumsum(sizes)[:-1].tolist()
    return jnp.split(x, idx, axis=-1)


def _rms_norm(x, g):
    xf = x.astype(jnp.float32)
    y = xf * lax.rsqrt(jnp.mean(xf * xf, axis=-1, keepdims=True) + NORM_EPS)
    return (y * g.astype(jnp.float32)).astype(x.dtype)


def _swiglu(x, g, w_in, w_out):
    h = _rms_norm(x, g)
    gate, up = jnp.split(h @ w_in, 2, axis=-1)
    return (jax.nn.silu(gate) * up) @ w_out


def _fox_attention(q, k, v, logf, past, blocks):
    if past is not None:
        pk, pv, plf = past
        k = jnp.concatenate([pk.astype(k.dtype), k], axis=1)
        v = jnp.concatenate([pv.astype(v.dtype), v], axis=1)
        logf = jnp.concatenate([plf.astype(jnp.float32), logf], axis=1)
    p0 = k.shape[1] - q.shape[1]
    c = jnp.cumsum(logf, axis=1).transpose(0, 2, 1)
    scale = HEAD_DIM ** -0.5
    outs = []
    for s0, s1 in blocks:
        n_k = p0 + s1
        s = jnp.einsum('bqhd,bkhd->bhqk', q[:, s0:s1], k[:, :n_k],
                       preferred_element_type=jnp.float32) * scale
        s = s + c[:, :, p0 + s0:p0 + s1, None] - c[:, :, None, :n_k]
        mask = jnp.arange(n_k)[None, :] <= jnp.arange(p0 + s0, p0 + s1)[:, None]
        s = jnp.where(mask, s, -jnp.inf)
        prob = jax.nn.softmax(s, axis=-1).astype(v.dtype)
        outs.append(jnp.einsum('bhqk,bkhd->bqhd', prob, v[:, :n_k]))
    return jnp.concatenate(outs, axis=1)


def _rwkv7(rw, shift0, S0, p, l):
    B, T, _ = rw.shape
    f32 = jnp.float32
    prev = jnp.concatenate([shift0[:, None].astype(rw.dtype), rw[:, :-1]], axis=1)
    z = rw + (prev - rw) * p['rwkv_mu'][l]
    r, k, v, xw, xa, xg = _split(z, RW_SIZES)
    w = -jax.nn.softplus(-(p['rwkv_w0'][l] + jnp.tanh(xw) @ p['rwkv_w2'][l])) - 0.5
    decay = jnp.exp(-jnp.exp(w.astype(f32)))
    a = jax.nn.sigmoid(p['rwkv_a0'][l] + xa @ p['rwkv_a2'][l])
    g = jax.nn.sigmoid(xg) @ p['rwkv_g2'][l]
    kk = k * p['rwkv_k_k'][l]
    k = k * (1.0 + (a - 1.0) * p['rwkv_k_a'][l])

    def heads(t):
        return t.reshape(B, T, RWKV_HEADS, RWKV_N).astype(f32)

    r, decay, k, v, kk, a = (heads(t) for t in (r, decay, k, v, kk, a))
    kk = kk / jnp.maximum(jnp.sqrt(jnp.sum(kk * kk, axis=-1, keepdims=True)), 1e-12)

    def step(S, inp):
        r_t, w_t, k_t, v_t, kk_t, a_t = inp
        sk = jnp.einsum('bhij,bhj->bhi', S, kk_t)
        S = (S * w_t[:, :, None, :] - sk[..., None] * (kk_t * a_t)[:, :, None, :]
             + v_t[..., None] * k_t[:, :, None, :])
        return S, jnp.einsum('bhij,bhj->bhi', S, r_t)

    xs = tuple(jnp.moveaxis(t, 1, 0) for t in (r, decay, k, v, kk, a))
    S_T, o = lax.scan(step, S0.astype(f32), xs)
    o = jnp.moveaxis(o, 0, 1)
    mu = jnp.mean(o, axis=-1, keepdims=True)
    var = jnp.mean(jnp.square(o - mu), axis=-1, keepdims=True)
    o = ((o - mu) * lax.rsqrt(var + LNX_EPS)).reshape(B, T, MIX_WIDTH)
    o = o * p['rwkv_lnx_w'][l].astype(f32) + p['rwkv_lnx_b'][l].astype(f32)
    bonus = jnp.sum(r * k * p['rwkv_r_k'][l].astype(f32), axis=-1, keepdims=True) * v
    o = o + bonus.reshape(B, T, MIX_WIDTH)
    return (o * g).astype(rw.dtype), S_T, rw[:, -1]


def _cplx_combine(e1, e2):
    a1r, a1i, b1r, b1i = e1
    a2r, a2i, b2r, b2i = e2
    return (a2r * a1r - a2i * a1i, a2r * a1i + a2i * a1r,
            a2r * b1r - a2i * b1i + b2r, a2r * b1i + a2i * b1r + b2i)


def _s5(u, x0_re, x0_im, p, l):
    B, T, _ = u.shape
    f32 = jnp.float32
    uf = u.astype(f32).reshape(B, T, S5_GROUPS, S5_GROUP)
    ar = p['s5_a_re'][l].astype(f32)
    ai = p['s5_a_im'][l].astype(f32)
    dt = jnp.exp(p['s5_log_dt'][l].astype(f32))[:, None]
    mag = jnp.exp(dt * ar)
    abar_re = mag * jnp.cos(dt * ai)
    abar_im = mag * jnp.sin(dt * ai)
    den = ar * ar + ai * ai
    nr, ni = abar_re - 1.0, abar_im
    z_re = (nr * ar + ni * ai) / den
    z_im = (ni * ar - nr * ai) / den
    b_re = p['s5_b_re'][l].astype(f32)
    b_im = p['s5_b_im'][l].astype(f32)
    bb_re = z_re[..., None] * b_re - z_im[..., None] * b_im
    bb_im = z_re[..., None] * b_im + z_im[..., None] * b_re
    bu_re = jnp.einsum('gpc,btgc->btgp', bb_re, uf)
    bu_im = jnp.einsum('gpc,btgc->btgp', bb_im, uf)
    x0r = x0_re.astype(f32)
    x0i = x0_im.astype(f32)
    bu_re = bu_re.at[:, 0].add(abar_re * x0r - abar_im * x0i)
    bu_im = bu_im.at[:, 0].add(abar_re * x0i + abar_im * x0r)
    a_full_re = jnp.broadcast_to(abar_re, bu_re.shape)
    a_full_im = jnp.broadcast_to(abar_im, bu_im.shape)
    _, _, xr, xi = lax.associative_scan(_cplx_combine, (a_full_re, a_full_im, bu_re, bu_im), axis=1)
    y = (jnp.einsum('gcp,btgp->btgc', p['s5_c_re'][l].astype(f32), xr)
         - jnp.einsum('gcp,btgp->btgc', p['s5_c_im'][l].astype(f32), xi)
         + p['s5_d'][l].astype(f32) * uf).reshape(B, T, MIX_WIDTH)
    y = jax.nn.gelu(y)
    out = y * jax.nn.sigmoid(y @ p['s5_glu_w'][l].astype(f32) + p['s5_glu_b'][l].astype(f32))
    return out.astype(u.dtype), xr[:, -1], xi[:, -1]


def _mixer(x, p, l, past, S0, sh0, re0, im0, blocks):
    B, T, _ = x.shape
    h = _rms_norm(x, p['mix_norm'][l])
    q, k, v, f_raw, rw, u, gates = _split(h @ p['w_in'][l], IN_SIZES)
    q = _rms_norm(q.reshape(B, T, FOX_HEADS, HEAD_DIM), p['fox_q_gain'][l])
    k = _rms_norm(k.reshape(B, T, FOX_HEADS, HEAD_DIM), p['fox_k_gain'][l])
    v = v.reshape(B, T, FOX_HEADS, HEAD_DIM)
    logf = jax.nn.log_sigmoid((f_raw + p['fox_b_f'][l]).astype(jnp.float32))
    o_a = _fox_attention(q, k, v, logf, past, blocks).reshape(B, T, MIX_WIDTH)
    o_b, S_T, sh_T = _rwkv7(rw, sh0, S0, p, l)
    o_c, s5_re, s5_im = _s5(u, re0, im0, p, l)
    w_a, w_b, w_c = jnp.split(p['w_branch'][l], N_BRANCH, axis=0)
    g_a, g_b, g_c = jnp.split(jax.nn.sigmoid(gates), N_BRANCH, axis=-1)
    merged = g_a * (o_a @ w_a) + g_b * (o_b @ w_b) + g_c * (o_c @ w_c)
    return merged @ p['w_out'][l], (k, v, logf, S_T, sh_T, s5_re, s5_im)


def _layer(x, p, l, past, S0, sh0, re0, im0, blocks):
    x = x + 0.5 * _swiglu(x, p['ffn1_norm'][l], p['ffn1_w_in'][l], p['ffn1_w_out'][l])
    mix, new_state = _mixer(x, p, l, past, S0, sh0, re0, im0, blocks)
    x = x + mix
    x = x + 0.5 * _swiglu(x, p['ffn2_norm'][l], p['ffn2_w_in'][l], p['ffn2_w_out'][l])
    return x, new_state


def setup_inputs(seed: int = 0) -> dict:
    key = jax.random.key(seed)
    keys = list(jax.random.split(key, 64))
    f32 = jnp.float32

    def nk():
        return keys.pop()

    def nrm(shape, scale=1.0):
        return scale * jax.random.normal(nk(), shape, f32)

    n_pages = PAST_LEN // PAGE_SIZE
    n_used = DEC_BATCH * n_pages
    n_phys = (n_used * POOL_NUM) // POOL_DEN
    page_table = jax.random.permutation(nk(), n_phys)[:n_used].reshape(DEC_BATCH, n_pages).astype(jnp.int32)
    chan = jnp.linspace(0.0, 1.0, MIX_WIDTH)
    return {
        'x_prompt': nrm((BATCH, SEQ, D_MODEL)),
        'x_sample': nrm((DEC_BATCH, DEC_SEQ, D_MODEL)),
        'cache_k': nrm((DEPTH, n_phys, PAGE_SIZE, FOX_HEADS, HEAD_DIM)),
        'cache_v': nrm((DEPTH, n_phys, PAGE_SIZE, FOX_HEADS, HEAD_DIM)),
        'cache_logf': jax.nn.log_sigmoid(nrm((DEPTH, n_phys, PAGE_SIZE, FOX_HEADS)) + 3.0),
        'page_table': page_table,
        'state_rwkv': nrm((DEPTH, DEC_BATCH, RWKV_HEADS, RWKV_N, RWKV_N)),
        'state_shift': nrm((DEPTH, DEC_BATCH, RW_COLS)),
        'state_s5_re': nrm((DEPTH, DEC_BATCH, S5_GROUPS, S5_STATE), 0.5),
        'state_s5_im': nrm((DEPTH, DEC_BATCH, S5_GROUPS, S5_STATE), 0.5),
        'meta_tokens': nrm((N_META, D_MODEL)),
        'ffn1_norm': 1.0 + nrm((DEPTH, D_MODEL), 0.02),
        'ffn1_w_in': nrm((DEPTH, D_MODEL, 2 * D_FF), D_MODEL ** -0.5),
        'ffn1_w_out': nrm((DEPTH, D_FF, D_MODEL), D_FF ** -0.5),
        'mix_norm': 1.0 + nrm((DEPTH, D_MODEL), 0.02),
        'w_in': nrm((DEPTH, D_MODEL, IN_COLS), D_MODEL ** -0.5),
        'fox_b_f': jnp.linspace(1.0, 6.0, FOX_HEADS)[None] + nrm((DEPTH, FOX_HEADS), 0.1),
        'fox_q_gain': 1.0 + nrm((DEPTH, HEAD_DIM), 0.02),
        'fox_k_gain': 1.0 + nrm((DEPTH, HEAD_DIM), 0.02),
        'rwkv_mu': jax.random.uniform(nk(), (DEPTH, RW_COLS), f32),
        'rwkv_w0': -6.5 + 5.0 * chan[None] ** 0.9 + nrm((DEPTH, MIX_WIDTH), 0.05),
        'rwkv_w2': nrm((DEPTH, RWKV_W_LORA, MIX_WIDTH), 0.1),
        'rwkv_a0': nrm((DEPTH, MIX_WIDTH), 0.1),
        'rwkv_a2': nrm((DEPTH, RWKV_A_LORA, MIX_WIDTH), 0.1),
        'rwkv_g2': nrm((DEPTH, RWKV_G_LORA, MIX_WIDTH), RWKV_G_LORA ** -0.5),
        'rwkv_k_k': 0.85 + nrm((DEPTH, MIX_WIDTH), 0.02),
        'rwkv_k_a': 1.0 + nrm((DEPTH, MIX_WIDTH), 0.02),
        'rwkv_r_k': nrm((DEPTH, RWKV_HEADS, RWKV_N), 0.1),
        'rwkv_lnx_w': 1.0 + nrm((DEPTH, MIX_WIDTH), 0.02),
        'rwkv_lnx_b': nrm((DEPTH, MIX_WIDTH), 0.02),
        's5_a_re': -0.5 + nrm((DEPTH, S5_GROUPS, S5_STATE), 0.01),
        's5_a_im': jnp.pi * jnp.arange(S5_STATE, dtype=f32) + nrm((DEPTH, S5_GROUPS, S5_STATE), 0.01),
        's5_log_dt': jax.random.uniform(nk(), (DEPTH, S5_GROUPS), f32, math.log(1e-3), math.log(1e-1)),
        's5_b_re': nrm((DEPTH, S5_GROUPS, S5_STATE, S5_GROUP), (2 * S5_GROUP) ** -0.5),
        's5_b_im': nrm((DEPTH, S5_GROUPS, S5_STATE, S5_GROUP), (2 * S5_GROUP) ** -0.5),
        's5_c_re': nrm((DEPTH, S5_GROUPS, S5_GROUP, S5_STATE), (2 * S5_STATE) ** -0.5),
        's5_c_im': nrm((DEPTH, S5_GROUPS, S5_GROUP, S5_STATE), (2 * S5_STATE) ** -0.5),
        's5_d': nrm((DEPTH, S5_GROUPS, S5_GROUP)),
        's5_glu_w': nrm((DEPTH, MIX_WIDTH, MIX_WIDTH), MIX_WIDTH ** -0.5),
        's5_glu_b': nrm((DEPTH, MIX_WIDTH), 0.02),
        'w_branch': nrm((DEPTH, N_BRANCH * MIX_WIDTH, D_MODEL), MIX_WIDTH ** -0.5),
        'w_out': nrm((DEPTH, D_MODEL, D_MODEL), D_MODEL ** -0.5),
        'ffn2_norm': 1.0 + nrm((DEPTH, D_MODEL), 0.02),
        'ffn2_w_in': nrm((DEPTH, D_MODEL, 2 * D_FF), D_MODEL ** -0.5),
        'ffn2_w_out': nrm((DEPTH, D_FF, D_MODEL), D_FF ** -0.5),
    }


def reference(x_prompt, x_sample, cache_k, cache_v, cache_logf, page_table, state_rwkv, state_shift,
              state_s5_re, state_s5_im, meta_tokens, ffn1_norm, ffn1_w_in, ffn1_w_out, mix_norm, w_in,
              fox_b_f, fox_q_gain, fox_k_gain, rwkv_mu, rwkv_w0, rwkv_w2, rwkv_a0, rwkv_a2, rwkv_g2,
              rwkv_k_k, rwkv_k_a, rwkv_r_k, rwkv_lnx_w, rwkv_lnx_b, s5_a_re, s5_a_im, s5_log_dt,
              s5_b_re, s5_b_im, s5_c_re, s5_c_im, s5_d, s5_glu_w, s5_glu_b, w_branch, w_out,
              ffn2_norm, ffn2_w_in, ffn2_w_out):
    p = dict(ffn1_norm=ffn1_norm, ffn1_w_in=ffn1_w_in, ffn1_w_out=ffn1_w_out, mix_norm=mix_norm,
             w_in=w_in, fox_b_f=fox_b_f, fox_q_gain=fox_q_gain, fox_k_gain=fox_k_gain,
             rwkv_mu=rwkv_mu, rwkv_w0=rwkv_w0, rwkv_w2=rwkv_w2, rwkv_a0=rwkv_a0, rwkv_a2=rwkv_a2,
             rwkv_g2=rwkv_g2, rwkv_k_k=rwkv_k_k, rwkv_k_a=rwkv_k_a, rwkv_r_k=rwkv_r_k,
             rwkv_lnx_w=rwkv_lnx_w, rwkv_lnx_b=rwkv_lnx_b, s5_a_re=s5_a_re, s5_a_im=s5_a_im,
             s5_log_dt=s5_log_dt, s5_b_re=s5_b_re, s5_b_im=s5_b_im, s5_c_re=s5_c_re, s5_c_im=s5_c_im,
             s5_d=s5_d, s5_glu_w=s5_glu_w, s5_glu_b=s5_glu_b, w_branch=w_branch, w_out=w_out,
             ffn2_norm=ffn2_norm, ffn2_w_in=ffn2_w_in, ffn2_w_out=ffn2_w_out)
    f32 = jnp.float32

    bp = x_prompt.shape[0]
    meta = jnp.broadcast_to(meta_tokens.astype(x_prompt.dtype)[None], (bp, N_META, D_MODEL))
    xp = jnp.concatenate([meta, x_prompt], axis=1)
    L = xp.shape[1]
    blocks_p = [(0, N_META)] + [(s, min(s + Q_BLOCK, L)) for s in range(N_META, L, Q_BLOCK)]
    zS = jnp.zeros((bp, RWKV_HEADS, RWKV_N, RWKV_N), f32)
    zsh = jnp.zeros((bp, RW_COLS), xp.dtype)
    zs5 = jnp.zeros((bp, S5_GROUPS, S5_STATE), f32)
    st_p = []
    for l in range(DEPTH):
        xp, st = _layer(xp, p, l, None, zS, zsh, zs5, zs5, blocks_p)
        st_p.append(st)
    y_prompt = xp[:, N_META:]

    db, t_new = x_sample.shape[0], x_sample.shape[1]
    blocks_s = [(s, min(s + Q_BLOCK, t_new)) for s in range(0, t_new, Q_BLOCK)]
    xs = x_sample
    st_s = []
    for l in range(DEPTH):
        past = (cache_k[l, page_table].reshape(db, -1, FOX_HEADS, HEAD_DIM),
                cache_v[l, page_table].reshape(db, -1, FOX_HEADS, HEAD_DIM),
                cache_logf[l, page_table].reshape(db, -1, FOX_HEADS))
        xs, st = _layer(xs, p, l, past, state_rwkv[l], state_shift[l], state_s5_re[l], state_s5_im[l], blocks_s)
        st_s.append(st)
    y_sample = xs

    def stack(states, i):
        return jnp.stack([s[i] for s in states], axis=0)

    return (y_prompt, y_sample,
            stack(st_p, 0), stack(st_p, 1), stack(st_p, 2), stack(st_p, 3), stack(st_p, 4), stack(st_p, 5), stack(st_p, 6),
            stack(st_s, 0), stack(st_s, 1), stack(st_s, 2), stack(st_s, 3), stack(st_s, 4), stack(st_s, 5), stack(st_s, 6))
```

```python
import functools

import jax
import jax.numpy as jnp
from jax import lax
from jax.experimental import pallas as pl
from jax.experimental.pallas import tpu as pltpu

F32 = jnp.float32
BF16 = jnp.bfloat16

D_MODEL = 1024
N_META = 16
HEAD_DIM = 64
MIX = D_MODEL // 2
N_HEADS = MIX // HEAD_DIM
D_FF = 2816
RW_COLS = 1824
RW_PAD = 1920
LORA_WA = 64
LORA_G = 160
S5_GROUPS = 32
S5_GROUP = 16
S5_STATE = 64
S5_LANES = S5_GROUPS * S5_STATE
QKVU_COLS = 4 * MIX + 128
PAGE = 128
NORM_EPS = 1e-6
LNX_EPS = 64e-5

LANE = 128
SEQ_ALIGN = 128
FF_TILE = 256
ATT_Q_TILE = 512
RWKV_CHUNK = 64
S5_CHUNK = 64
DECODE_PAGES_PER_STEP = 8
VMEM_LIMIT = 56 * 1024 * 1024

NN = (((1,), (0,)), ((), ()))
NT = (((1,), (1,)), ((), ()))
TN = (((0,), (0,)), ((), ()))


def _params(sem):
    return pltpu.CompilerParams(dimension_semantics=sem, vmem_limit_bytes=VMEM_LIMIT)


def _tile(n, target):
    best = None
    for t in range(8, min(n, target) + 1, 8):
        if n % t == 0:
            best = t
    return best or n


def _dot(a, b, dn=NN):
    return lax.dot_general(a, b, dn, preferred_element_type=F32)


def _split2(x):
    hi = x.astype(BF16)
    return hi, (x - hi.astype(F32)).astype(BF16)


def _dot3(a, b, dn=NN):
    ah, al = _split2(a)
    bh, bl = _split2(b)
    return _dot(ah, bh, dn) + _dot(ah, bl, dn) + _dot(al, bh, dn)


def _dot_exact_lhs(a01, b):
    a = a01.astype(BF16)
    b1 = b.astype(BF16)
    r1 = b - b1.astype(F32)
    b2 = r1.astype(BF16)
    b3 = (r1 - b2.astype(F32)).astype(BF16)
    return _dot(a, b1) + _dot(a, b2) + _dot(a, b3)


def _dot_exact_rhs(a, b01):
    b = b01.astype(BF16)
    a1 = a.astype(BF16)
    r1 = a - a1.astype(F32)
    a2 = r1.astype(BF16)
    a3 = (r1 - a2.astype(F32)).astype(BF16)
    return _dot(a1, b) + _dot(a2, b) + _dot(a3, b)


def _segsum(x, ones_bd):
    hi, lo = _split2(x)
    return _dot(hi, ones_bd) + _dot(lo, ones_bd)


def _softplus(x):
    return jnp.maximum(x, 0.0) + jnp.log1p(jnp.exp(-jnp.abs(x)))


def _rms(x, g):
    ms = jnp.mean(x * x, axis=-1, keepdims=True)
    return x * lax.rsqrt(ms + NORM_EPS) * g


def _ffn_kernel(x_ref, g_ref, wg_ref, wu_ref, wo_ref, gn_ref, *rest, emit_hn):
    if emit_hn:
        o_ref, hn_ref, xn_s, acc_s = rest
    else:
        o_ref, xn_s, acc_s = rest
    j = pl.program_id(1)

    @pl.when(j == 0)
    def _():
        xn_s[...] = _rms(x_ref[...], g_ref[...]).astype(BF16)
        acc_s[...] = jnp.zeros_like(acc_s)

    xn = xn_s[...]
    gate = _dot(xn, wg_ref[...])
    up = _dot(xn, wu_ref[...])
    act = (gate * jax.nn.sigmoid(gate) * up).astype(BF16)
    acc_s[...] += _dot(act, wo_ref[...])

    @pl.when(j == pl.num_programs(1) - 1)
    def _():
        y = x_ref[...] + 0.5 * acc_s[...]
        o_ref[...] = y
        if emit_hn:
            hn_ref[...] = _rms(y, gn_ref[...]).astype(BF16)


def _ffn(x, g, w_in, w_out, g_next, emit_hn):
    rows = x.shape[0]
    tm = _tile(rows, 1024)
    nf = D_FF // FF_TILE
    out_shape = [jax.ShapeDtypeStruct((rows, D_MODEL), F32)]
    out_specs = [pl.BlockSpec((tm, D_MODEL), lambda i, j: (i, 0))]
    if emit_hn:
        out_shape.append(jax.ShapeDtypeStruct((rows, D_MODEL), BF16))
        out_specs.append(pl.BlockSpec((tm, D_MODEL), lambda i, j: (i, 0)))
    res = pl.pallas_call(
        functools.partial(_ffn_kernel, emit_hn=emit_hn),
        grid=(rows // tm, nf),
        in_specs=[
            pl.BlockSpec((tm, D_MODEL), lambda i, j: (i, 0)),
            pl.BlockSpec((1, D_MODEL), lambda i, j: (0, 0)),
            pl.BlockSpec((D_MODEL, FF_TILE), lambda i, j: (0, j)),
            pl.BlockSpec((D_MODEL, FF_TILE), lambda i, j: (0, j + nf)),
            pl.BlockSpec((FF_TILE, D_MODEL), lambda i, j: (j, 0)),
            pl.BlockSpec((1, D_MODEL), lambda i, j: (0, 0)),
        ],
        out_specs=out_specs,
        out_shape=out_shape,
        scratch_shapes=[pltpu.VMEM((tm, D_MODEL), BF16), pltpu.VMEM((tm, D_MODEL), F32)],
        compiler_params=_params(("parallel", "arbitrary")),
    )(x, g, w_in, w_in, w_out, g_next)
    return res if emit_hn else (res[0], None)


def _matmul_kernel(a_ref, b_ref, o_ref):
    o_ref[...] = _dot(a_ref[...], b_ref[...])


def _matmul(a, b, tn):
    rows, k = a.shape
    n = b.shape[1]
    tm = _tile(rows, 1024)
    return pl.pallas_call(
        _matmul_kernel,
        grid=(rows // tm, n // tn),
        in_specs=[pl.BlockSpec((tm, k), lambda i, j: (i, 0)), pl.BlockSpec((k, tn), lambda i, j: (0, j))],
        out_specs=pl.BlockSpec((tm, tn), lambda i, j: (i, j)),
        out_shape=jax.ShapeDtypeStruct((rows, n), F32),
        compiler_params=_params(("parallel", "arbitrary")),
    )(a, b)


def _qk_prep_kernel(q_ref, k_ref, f_ref, qg_ref, kg_ref, bf_ref, ones_ref, qn_ref, kn_ref, lf_ref, lfp_ref):
    ones_bd = ones_ref[...]

    def headnorm(x, g):
        ms = _segsum(x * x, ones_bd) * (1.0 / HEAD_DIM)
        return x * lax.rsqrt(ms + NORM_EPS) * g

    qn_ref[...] = (headnorm(q_ref[...], qg_ref[...]) * (HEAD_DIM ** -0.5)).astype(BF16)
    kn_ref[...] = headnorm(k_ref[...], kg_ref[...])
    lf = -_softplus(-(f_ref[...] + bf_ref[...]))
    lf_ref[...] = lf
    lane = lax.broadcasted_iota(jnp.int32, lf.shape, 1)
    for hp in range(N_HEADS // 2):
        a = lf[:, 2 * hp:2 * hp + 1]
        b = lf[:, 2 * hp + 1:2 * hp + 2]
        lfp_ref[:, hp * LANE:(hp + 1) * LANE] = jnp.where(lane == 0, a, jnp.where(lane == 1, b, 0.0))


def _qk_prep(qkvu, qg, kg, bf, ones_bd):
    rows = qkvu.shape[0]
    tm = _tile(rows, 1024)
    small = lambda w: pl.BlockSpec((1, w), lambda i: (0, 0))
    return pl.pallas_call(
        _qk_prep_kernel,
        grid=(rows // tm,),
        in_specs=[
            pl.BlockSpec((tm, MIX), lambda i: (i, 0)),
            pl.BlockSpec((tm, MIX), lambda i: (i, 1)),
            pl.BlockSpec((tm, LANE), lambda i: (i, 4 * MIX // LANE)),
            small(MIX), small(MIX), small(LANE),
            pl.BlockSpec((MIX, MIX), lambda i: (0, 0)),
        ],
        out_specs=[
            pl.BlockSpec((tm, MIX), lambda i: (i, 0)),
            pl.BlockSpec((tm, MIX), lambda i: (i, 0)),
            pl.BlockSpec((tm, LANE), lambda i: (i, 0)),
            pl.BlockSpec((tm, MIX), lambda i: (i, 0)),
        ],
        out_shape=[
            jax.ShapeDtypeStruct((rows, MIX), BF16),
            jax.ShapeDtypeStruct((rows, MIX), F32),
            jax.ShapeDtypeStruct((rows, LANE), F32),
            jax.ShapeDtypeStruct((rows, MIX), F32),
        ],
        compiler_params=_params(("parallel",)),
    )(qkvu, qkvu, qkvu, qg, kg, bf, ones_bd)


def _fox_attn_kernel(q_ref, k_ref, v_ref, lf_ref, o_ref, c_s):
    tp = q_ref.shape[0]
    cb = LANE
    row = lax.broadcasted_iota(jnp.int32, (cb, cb), 0)
    col = lax.broadcasted_iota(jnp.int32, (cb, cb), 1)
    tri = (row >= col).astype(F32)
    carry = jnp.zeros((1, LANE), F32)
    for blk in range(tp // cb):
        cblk = _dot_exact_lhs(tri, lf_ref[blk * cb:(blk + 1) * cb, :]) + carry
        c_s[blk * cb:(blk + 1) * cb, :] = cblk
        carry = cblk[cb - 1:cb, :]
    c = c_s[...]
    c_t = c.T
    q = q_ref[...]
    k = k_ref[...].astype(BF16)
    v = v_ref[...].astype(BF16)
    lane = lax.broadcasted_iota(jnp.int32, (1, LANE), 1)
    for r0 in range(0, tp, ATT_Q_TILE):
        r1 = min(r0 + ATT_Q_TILE, tp)
        tq = r1 - r0
        qi = lax.broadcasted_iota(jnp.int32, (tq, r1), 0) + r0
        ki = lax.broadcasted_iota(jnp.int32, (tq, r1), 1)
        causal = ki <= qi
        outs = []
        for e in range(2):
            head = (lane // HEAD_DIM) == e
            qe = jnp.where(head, q[r0:r1, :], jnp.zeros((), BF16))
            s = _dot(qe, k[:r1, :], NT)
            s = s + c[r0:r1, e:e + 1] - c_t[e:e + 1, :r1]
            s = jnp.where(causal, s, -1e30)
            m = jnp.max(s, axis=-1, keepdims=True)
            p = jnp.exp(s - m)
            l = jnp.sum(p, axis=-1, keepdims=True)
            outs.append(_dot(p.astype(BF16), v[:r1, :]) / l)
        o_ref[r0:r1, :] = jnp.where((lane // HEAD_DIM) == 0, outs[0], outs[1])


def _fox_attention(qn, kn, qkvu, lfp, bsz, tp):
    hp = N_HEADS // 2
    vblk = QKVU_COLS // LANE
    out = pl.pallas_call(
        _fox_attn_kernel,
        grid=(bsz, hp),
        in_specs=[
            pl.BlockSpec((tp, LANE), lambda b, h: (0, b * hp + h)),
            pl.BlockSpec((tp, LANE), lambda b, h: (0, b * hp + h)),
            pl.BlockSpec((tp, LANE), lambda b, h: (0, b * vblk + 2 * MIX // LANE + h)),
            pl.BlockSpec((tp, LANE), lambda b, h: (0, b * hp + h)),
        ],
        out_specs=pl.BlockSpec((tp, LANE), lambda b, h: (0, b * hp + h)),
        out_shape=jax.ShapeDtypeStruct((tp, bsz * MIX), F32),
        scratch_shapes=[pltpu.VMEM((tp, LANE), F32)],
        compiler_params=_params(("parallel", "parallel")),
    )(qn.reshape(tp, bsz * MIX), kn.reshape(tp, bsz * MIX), qkvu.reshape(tp, bsz * QKVU_COLS),
      lfp.reshape(tp, bsz * MIX))
    return out.reshape(tp * bsz, MIX)


def _fox_decode_kernel(pt_ref, q_ref, kn_ref, vn_ref, lfn_ref, *rest, pages_per_step):
    del pt_ref
    g = pages_per_step
    k_refs, v_refs, lf_refs = rest[:g], rest[g:2 * g], rest[2 * g:3 * g]
    o_ref, m_s, l_s, acc_s, carry_s = rest[3 * g:]
    p = pl.program_id(1)
    hrow = lax.broadcasted_iota(jnp.int32, (N_HEADS, MIX), 0)
    hlane = lax.broadcasted_iota(jnp.int32, (N_HEADS, MIX), 1) // HEAD_DIM
    headmask = hrow == hlane
    qb = jnp.where(headmask, q_ref[...], 0.0)

    @pl.when(p == 0)
    def _():
        m_s[...] = jnp.sum(qb * kn_ref[...], axis=1, keepdims=True)
        l_s[...] = jnp.ones_like(l_s)
        acc_s[...] = jnp.where(headmask, vn_ref[...], 0.0)
        carry_s[...] = lfn_ref[...]

    srow = lax.broadcasted_iota(jnp.int32, (PAGE, PAGE), 0)
    scol = lax.broadcasted_iota(jnp.int32, (PAGE, PAGE), 1)
    later = (srow > scol).astype(F32)
    qbb = qb.astype(BF16)
    m, l, acc, carry = m_s[...], l_s[...], acc_s[...], carry_s[...]
    for i in range(g):
        lf = lf_refs[i][...]
        bias = carry + _dot_exact_rhs(lf, later)
        s = _dot(qbb, k_refs[i][...].astype(BF16), NT) + bias
        m_new = jnp.maximum(m, jnp.max(s, axis=1, keepdims=True))
        alpha = jnp.exp(m - m_new)
        pr = jnp.exp(s - m_new)
        l = l * alpha + jnp.sum(pr, axis=1, keepdims=True)
        acc = acc * alpha + _dot(pr.astype(BF16), v_refs[i][...].astype(BF16))
        m = m_new
        carry = carry + jnp.sum(lf, axis=1, keepdims=True)
    m_s[...], l_s[...], acc_s[...], carry_s[...] = m, l, acc, carry

    @pl.when(p == pl.num_programs(1) - 1)
    def _():
        o_ref[...] = jnp.sum(jnp.where(headmask, acc / l, 0.0), axis=0, keepdims=True)


def _fox_decode(layer, q, kn, vn, lfn, cache_k, cache_v, cache_lft, page_table):
    bsz, n_pages = page_table.shape
    g = DECODE_PAGES_PER_STEP
    assert n_pages % g == 0

    def page_spec(width_block, i):
        return pl.BlockSpec((None, None) + width_block,
                            lambda b, p, pt: (layer, pt[b, n_pages - 1 - (p * g + i)], 0, 0))

    tok = lambda w: pl.BlockSpec((None, 1, w), lambda b, p, pt: (b, 0, 0))
    in_specs = [tok(MIX), tok(MIX), tok(MIX), pl.BlockSpec((None, N_HEADS, 1), lambda b, p, pt: (b, 0, 0))]
    in_specs += [page_spec((PAGE, MIX), i) for i in range(g)]
    in_specs += [page_spec((PAGE, MIX), i) for i in range(g)]
    in_specs += [page_spec((N_HEADS, PAGE), i) for i in range(g)]
    grid_spec = pltpu.PrefetchScalarGridSpec(
        num_scalar_prefetch=1,
        grid=(bsz, n_pages // g),
        in_specs=in_specs,
        out_specs=pl.BlockSpec((None, 1, MIX), lambda b, p, pt: (b, 0, 0)),
        scratch_shapes=[pltpu.VMEM((N_HEADS, 1), F32), pltpu.VMEM((N_HEADS, 1), F32),
                        pltpu.VMEM((N_HEADS, MIX), F32), pltpu.VMEM((N_HEADS, 1), F32)],
    )
    out = pl.pallas_call(
        functools.partial(_fox_decode_kernel, pages_per_step=g),
        grid_spec=grid_spec,
        out_shape=jax.ShapeDtypeStruct((bsz, 1, MIX), F32),
        compiler_params=_params(("parallel", "arbitrary")),
    )(page_table, q[:, None], kn[:, None], vn[:, None], lfn[:, :N_HEADS, None],
      *([cache_k] * g), *([cache_v] * g), *([cache_lft] * g))
    return out[:, 0]


def _rwkv_prep_kernel(rw_ref, prev_ref, sh0_ref, mu_ref, w0_ref, w2_ref, a0_ref, a2_ref, g2_ref,
                      kk_ref, ka_ref, rk_ref, ones_ref,
                      r_out, lw_out, k_out, v_out, kk_out, b_out, g_out, bonus_out, *, bsz):
    i = pl.program_id(0)
    rw = rw_ref[...]
    tm = rw.shape[0]
    first = jnp.where(i == 0, sh0_ref[...], prev_ref[...])
    prev = jnp.concatenate([first, rw[:tm - bsz, :]], axis=0) if tm > bsz else first
    z = rw + (prev - rw) * mu_ref[...]
    r = z[:, 0:MIX]
    k = z[:, MIX:2 * MIX]
    v = z[:, 2 * MIX:3 * MIX]
    xwa = z[:, 3 * MIX:3 * MIX + 2 * LORA_WA]
    xg = z[:, 3 * MIX + 2 * LORA_WA:]
    ones_bd = ones_ref[...]
    w = -_softplus(-(w0_ref[...] + _dot(jnp.tanh(xwa).astype(BF16), w2_ref[...]))) - 0.5
    lw_out[...] = -jnp.exp(w)
    a = jax.nn.sigmoid(a0_ref[...] + _dot(xwa.astype(BF16), a2_ref[...]))
    g_out[...] = _dot(jax.nn.sigmoid(xg).astype(BF16), g2_ref[...])
    kk = k * kk_ref[...]
    kk = kk / jnp.maximum(jnp.sqrt(_segsum(kk * kk, ones_bd)), 1e-12)
    k2 = k * (1.0 + (a - 1.0) * ka_ref[...])
    r_out[...] = r
    k_out[...] = k2
    v_out[...] = v
    kk_out[...] = kk
    b_out[...] = kk * a
    bonus_out[...] = _segsum(r * k2 * rk_ref[...], ones_bd) * v


def _rwkv_prep(rwp, shift0, prm, bsz):
    rows = rwp.shape[0]
    tm = _tile(rows, 512)
    assert tm % bsz == 0
    nb = tm // bsz
    small = lambda a: pl.BlockSpec(a.shape, lambda i: (0, 0))
    consts = [prm["mu"], prm["w0"], prm["w2"], prm["a0"], prm["a2"], prm["g2"], prm["k_k"], prm["k_a"],
              prm["r_k"], prm["ones_bd"]]
    outs = pl.pallas_call(
        functools.partial(_rwkv_prep_kernel, bsz=bsz),
        grid=(rows // tm,),
        in_specs=[
            pl.BlockSpec((tm, RW_PAD), lambda i: (i, 0)),
            pl.BlockSpec((bsz, RW_PAD), lambda i: (jnp.maximum(i * nb - 1, 0), 0)),
            pl.BlockSpec((bsz, RW_PAD), lambda i: (0, 0)),
        ] + [small(c) for c in consts],
        out_specs=[pl.BlockSpec((tm, MIX), lambda i: (i, 0))] * 8,
        out_shape=[jax.ShapeDtypeStruct((rows, MIX), F32)] * 8,
        compiler_params=_params(("parallel",)),
    )(rwp, rwp, shift0, *consts)
    return outs


def _rwkv_scan_kernel(r_ref, lw_ref, k_ref, v_ref, kk_ref, b_ref, s0_ref, o_ref, s_out, s_s, *, chunk, t_valid):
    c = pl.program_id(1)
    n = chunk

    @pl.when(c == 0)
    def _():
        s_s[...] = s0_ref[...]

    trow = lax.broadcasted_iota(jnp.int32, (n, 1), 0) + c * n
    valid = trow < t_valid
    lw = jnp.where(valid, lw_ref[...], 0.0)
    kk = jnp.where(valid, kk_ref[...], 0.0)
    bb = jnp.where(valid, b_ref[...], 0.0)
    kx = jnp.where(valid, k_ref[...], 0.0)
    vx = jnp.where(valid, v_ref[...], 0.0)
    rx = r_ref[...]

    row = lax.broadcasted_iota(jnp.int32, (n, n), 0)
    col = lax.broadcasted_iota(jnp.int32, (n, n), 1)
    incl = row >= col
    strict = row > col
    eye = (row == col).astype(F32)
    ci = _dot_exact_lhs(incl.astype(F32), lw)
    ce = ci - lw
    kkd = kk * jnp.exp(ce)
    rd = rx * jnp.exp(ci)
    einv = jnp.exp(-ci)
    bd = bb * einv
    kd = kx * einv
    cl = ci[n - 1:n, :]
    wl = jnp.exp(cl)
    dl = jnp.exp(cl - ci)
    bdw = bb * dl
    kdw = kx * dl
    row2 = lax.broadcasted_iota(jnp.int32, (n, 2 * n), 0)
    col2 = lax.broadcasted_iota(jnp.int32, (n, 2 * n), 1)
    col2 = jnp.where(col2 >= n, col2 - n, col2)
    strict2 = row2 > col2
    incl2 = row2 >= col2

    outs = []
    for h in range(N_HEADS):
        sl = slice(h * HEAD_DIM, (h + 1) * HEAD_DIM)
        a_m = jnp.concatenate([kkd[:, sl], rd[:, sl]], axis=0)
        b_m = jnp.concatenate([bd[:, sl], kd[:, sl]], axis=0)
        pm = _dot3(a_m, b_m, NT)
        p_top = jnp.where(strict2, pm[:n, :], 0.0)
        p_bot = jnp.where(incl2, pm[n:, :], 0.0)
        s_h = s_s[h]
        gm = _dot3(a_m, s_h, NT)
        x = -p_top[:, :n]
        tinv = eye + x
        span = 2
        while span < n:
            x = _dot3(x, x)
            tinv = tinv + _dot3(tinv, x)
            span *= 2
        v_h = vx[:, sl]
        u = -_dot3(tinv, gm[:n, :] + _dot3(p_top[:, n:], v_h))
        uv = jnp.concatenate([u, v_h], axis=0)
        outs.append(gm[n:, :] + _dot3(p_bot, uv))
        bk = jnp.concatenate([bdw[:, sl], kdw[:, sl]], axis=0)
        s_s[h] = s_h * wl[:, sl] + _dot3(uv, bk, TN)
    o_ref[...] = jnp.concatenate(outs, axis=1)

    @pl.when(c == pl.num_programs(1) - 1)
    def _():
        s_out[...] = s_s[...]


def _rwkv_scan(r, lw, k, v, kk, b, s0, bsz, tp, t_valid, chunk):
    view = lambda a: a.reshape(tp, bsz * MIX)
    blk = pl.BlockSpec((chunk, MIX), lambda bi, c: (c, bi))
    st = pl.BlockSpec((None, N_HEADS, HEAD_DIM, HEAD_DIM), lambda bi, c: (bi, 0, 0, 0))
    o, s_out = pl.pallas_call(
        functools.partial(_rwkv_scan_kernel, chunk=chunk, t_valid=t_valid),
        grid=(bsz, tp // chunk),
        in_specs=[blk] * 6 + [st],
        out_specs=[blk, st],
        out_shape=[jax.ShapeDtypeStruct((tp, bsz * MIX), F32),
                   jax.ShapeDtypeStruct((bsz, N_HEADS, HEAD_DIM, HEAD_DIM), F32)],
        scratch_shapes=[pltpu.VMEM((N_HEADS, HEAD_DIM, HEAD_DIM), F32)],
        compiler_params=_params(("parallel", "arbitrary")),
    )(view(r), view(lw), view(k), view(v), view(kk), view(b), s0)
    return o.reshape(tp * bsz, MIX), s_out


def _s5_kernel(u_ref, x0_ref, a_ref, bm_ref, cm_ref, d_ref, gw_ref, gb_ref, o_ref, xo_ref, buf_s, st_s,
               *, t_valid):
    c = pl.program_id(0)
    tc, bsz, _ = u_ref.shape
    n2 = 2 * S5_LANES

    @pl.when(c == 0)
    def _():
        st_s[...] = x0_ref[...]

    u2 = u_ref[...].reshape(tc * bsz, MIX)
    ub = u2.astype(BF16)
    cw = 1024
    for j in range(n2 // cw):
        buf_s[:, :, j * cw:(j + 1) * cw] = _dot(ub, bm_ref[:, j * cw:(j + 1) * cw]).reshape(tc, bsz, cw)

    lw = max(LANE, n2 // bsz)
    for lg in range(S5_LANES // lw):
        re = pl.ds(lg * lw, lw)
        im = pl.ds(S5_LANES + lg * lw, lw)
        ar = jnp.broadcast_to(a_ref[:, re], (bsz, lw))
        ai = jnp.broadcast_to(a_ref[:, im], (bsz, lw))

        def body(t, carry, re=re, im=im, ar=ar, ai=ai):
            xr, xi = carry
            nxr = ar * xr - ai * xi + buf_s[t, :, re]
            nxi = ar * xi + ai * xr + buf_s[t, :, im]
            buf_s[t, :, re] = nxr
            buf_s[t, :, im] = nxi
            return nxr, nxi

        xr, xi = lax.fori_loop(0, tc, body, (st_s[:, re], st_s[:, im]))
        st_s[:, re] = xr
        st_s[:, im] = xi

    c_valid, t_in = divmod(t_valid - 1, tc)

    @pl.when(c == c_valid)
    def _():
        xo_ref[...] = buf_s[t_in]

    xs = buf_s[...].reshape(tc * bsz, n2).astype(BF16)
    y = _dot(xs, cm_ref[...]) + d_ref[...] * u2
    y = 0.5 * y * (1.0 + jnp.tanh(0.7978845608028654 * (y + 0.044715 * (y * y * y))))
    gate = jax.nn.sigmoid(_dot(y.astype(BF16), gw_ref[...]) + gb_ref[...])
    o_ref[...] = (y * gate).reshape(tc, bsz, MIX)


def _s5(qkvu, x0, prm, bsz, tp, t_valid, tc):
    n2 = 2 * S5_LANES
    full = lambda a: pl.BlockSpec(a.shape, lambda c: (0,) * a.ndim)
    consts = [prm["s5_a"], prm["s5_bm"], prm["s5_cm"], prm["s5_d"], prm["s5_gw"], prm["s5_gb"]]
    o, xo = pl.pallas_call(
        functools.partial(_s5_kernel, t_valid=t_valid),
        grid=(tp // tc,),
        in_specs=[pl.BlockSpec((tc, bsz, MIX), lambda c: (c, 0, 3)), full(x0)] + [full(a) for a in consts],
        out_specs=[pl.BlockSpec((tc, bsz, MIX), lambda c: (c, 0, 0)), pl.BlockSpec((bsz, n2), lambda c: (0, 0))],
        out_shape=[jax.ShapeDtypeStruct((tp, bsz, MIX), F32), jax.ShapeDtypeStruct((bsz, n2), F32)],
        scratch_shapes=[pltpu.VMEM((tc, bsz, n2), F32), pltpu.VMEM((bsz, n2), F32)],
        compiler_params=_params(("arbitrary",)),
    )(qkvu.reshape(tp, bsz, QKVU_COLS), x0, *consts)
    return o.reshape(tp * bsz, MIX), xo


def _merge_kernel(x_ref, oa_ref, ob_ref, g_ref, bonus_ref, oc_ref, gates_ref, lw_ref, lb_ref, ones_ref,
                  wb_ref, wo_ref, o_ref):
    ones_bd = ones_ref[...]
    o = ob_ref[...]
    mu = _segsum(o, ones_bd) * (1.0 / HEAD_DIM)
    dlt = o - mu
    var = _segsum(dlt * dlt, ones_bd) * (1.0 / HEAD_DIM)
    ob = (dlt * lax.rsqrt(var + LNX_EPS) * lw_ref[...] + lb_ref[...] + bonus_ref[...]) * g_ref[...]
    merged = jax.nn.sigmoid(gates_ref[:, 0:D_MODEL]) * _dot(oa_ref[...].astype(BF16), wb_ref[0:MIX, :])
    merged += jax.nn.sigmoid(gates_ref[:, D_MODEL:2 * D_MODEL]) * _dot(ob.astype(BF16), wb_ref[MIX:2 * MIX, :])
    merged += jax.nn.sigmoid(gates_ref[:, 2 * D_MODEL:]) * _dot(oc_ref[...].astype(BF16), wb_ref[2 * MIX:, :])
    o_ref[...] = x_ref[...] + _dot(merged.astype(BF16), wo_ref[...])


def _merge(x, oa, ob, g, bonus, oc, gates, prm):
    rows = x.shape[0]
    tm = _tile(rows, 512)
    row = lambda w: pl.BlockSpec((tm, w), lambda i: (i, 0))
    full = lambda a: pl.BlockSpec(a.shape, lambda i: (0, 0))
    consts = [prm["lnx_w"], prm["lnx_b"], prm["ones_bd"], prm["w_branch"], prm["w_out"]]
    return pl.pallas_call(
        _merge_kernel,
        grid=(rows // tm,),
        in_specs=[row(D_MODEL), row(MIX), row(MIX), row(MIX), row(MIX), row(MIX), row(3 * D_MODEL)]
        + [full(c) for c in consts],
        out_specs=row(D_MODEL),
        out_shape=jax.ShapeDtypeStruct((rows, D_MODEL), F32),
        compiler_params=_params(("parallel",)),
    )(x, oa, ob, g, bonus, oc, gates, *consts)


def _layer_params(l, p):
    bf = lambda a: a.astype(BF16)
    row = lambda a, w=None: (a if w is None else jnp.pad(a, (0, w - a.shape[0]))).reshape(1, -1).astype(F32)
    w_in = p["w_in"][l]
    o_f = 3 * MIX
    o_rw = o_f + N_HEADS
    o_u = o_rw + RW_COLS
    o_g = o_u + MIX
    w_qkvu = jnp.concatenate(
        [w_in[:, :o_f], w_in[:, o_u:o_g], jnp.pad(w_in[:, o_f:o_rw], ((0, 0), (0, LANE - N_HEADS)))], axis=1)
    w_rw = jnp.pad(w_in[:, o_rw:o_u], ((0, 0), (0, RW_PAD - RW_COLS)))
    w_gates = w_in[:, o_g:]

    ar = p["s5_a_re"][l].astype(F32)
    ai = p["s5_a_im"][l].astype(F32)
    dt = jnp.exp(p["s5_log_dt"][l].astype(F32))[:, None]
    mag = jnp.exp(dt * ar)
    abar_re = mag * jnp.cos(dt * ai)
    abar_im = mag * jnp.sin(dt * ai)
    den = ar * ar + ai * ai
    nr, ni = abar_re - 1.0, abar_im
    z_re = (nr * ar + ni * ai) / den
    z_im = (ni * ar - nr * ai) / den
    b_re = p["s5_b_re"][l].astype(F32)
    b_im = p["s5_b_im"][l].astype(F32)
    bb_re = z_re[..., None] * b_re - z_im[..., None] * b_im
    bb_im = z_re[..., None] * b_im + z_im[..., None] * b_re
    eye = jnp.eye(S5_GROUPS, dtype=F32)
    blk_in = lambda m: jnp.einsum("gpc,gh->gchp", m, eye).reshape(MIX, S5_LANES)
    blk_out = lambda m: jnp.einsum("gcp,gh->gphc", m, eye).reshape(S5_LANES, MIX)
    s5_bm = jnp.concatenate([blk_in(bb_re), blk_in(bb_im)], axis=1)
    s5_cm = jnp.concatenate([blk_out(p["s5_c_re"][l].astype(F32)), -blk_out(p["s5_c_im"][l].astype(F32))], axis=0)

    hd = lax.broadcasted_iota(jnp.int32, (MIX, MIX), 0) // HEAD_DIM
    hd_t = lax.broadcasted_iota(jnp.int32, (MIX, MIX), 1) // HEAD_DIM
    return dict(
        ffn1_norm=row(p["ffn1_norm"][l]), ffn1_w_in=bf(p["ffn1_w_in"][l]), ffn1_w_out=bf(p["ffn1_w_out"][l]),
        ffn2_norm=row(p["ffn2_norm"][l]), ffn2_w_in=bf(p["ffn2_w_in"][l]), ffn2_w_out=bf(p["ffn2_w_out"][l]),
        mix_norm=row(p["mix_norm"][l]),
        w_qkvu=bf(w_qkvu), w_rw=bf(w_rw), w_gates=bf(w_gates),
        q_gain=row(jnp.tile(p["fox_q_gain"][l], N_HEADS)), k_gain=row(jnp.tile(p["fox_k_gain"][l], N_HEADS)),
        b_f=row(p["fox_b_f"][l], LANE),
        ones_bd=(hd == hd_t).astype(BF16),
        mu=row(p["rwkv_mu"][l], RW_PAD), w0=row(p["rwkv_w0"][l]), a0=row(p["rwkv_a0"][l]),
        w2=bf(jnp.pad(p["rwkv_w2"][l], ((0, LORA_WA), (0, 0)))),
        a2=bf(jnp.pad(p["rwkv_a2"][l], ((LORA_WA, 0), (0, 0)))),
        g2=bf(jnp.pad(p["rwkv_g2"][l], ((0, RW_PAD - RW_COLS), (0, 0)))),
        k_k=row(p["rwkv_k_k"][l]), k_a=row(p["rwkv_k_a"][l]), r_k=row(p["rwkv_r_k"][l].reshape(-1)),
        lnx_w=row(p["rwkv_lnx_w"][l]), lnx_b=row(p["rwkv_lnx_b"][l]),
        s5_a=jnp.concatenate([abar_re.reshape(1, -1), abar_im.reshape(1, -1)], axis=1),
        s5_bm=bf(s5_bm), s5_cm=bf(s5_cm), s5_d=row(p["s5_d"][l].reshape(-1)),
        s5_gw=bf(p["s5_glu_w"][l]), s5_gb=row(p["s5_glu_b"][l]),
        w_branch=bf(p["w_branch"][l]), w_out=bf(p["w_out"][l]),
    )


def _mixer_inputs(x, prm):
    x, hn = _ffn(x, prm["ffn1_norm"], prm["ffn1_w_in"], prm["ffn1_w_out"], prm["mix_norm"], True)
    qkvu = _matmul(hn, prm["w_qkvu"], QKVU_COLS)
    rwp = _matmul(hn, prm["w_rw"], RW_PAD)
    gates = _matmul(hn, prm["w_gates"], D_MODEL)
    qn, kn, lf, lfp = _qk_prep(qkvu, prm["q_gain"], prm["k_gain"], prm["b_f"], prm["ones_bd"])
    return x, qkvu, rwp, gates, qn, kn, lf, lfp


def _layer_tail(x, oa, qkvu, rwp, gates, prm, shift0, s0, x0, bsz, tp, t_valid, chunk, tc):
    r, lw, k2, v, kk, b, g, bonus = _rwkv_prep(rwp, shift0, prm, bsz)
    ob, s_out = _rwkv_scan(r, lw, k2, v, kk, b, s0, bsz, tp, t_valid, chunk)
    oc, xo = _s5(qkvu, x0, prm, bsz, tp, t_valid, tc)
    x = _merge(x, oa, ob, g, bonus, oc, gates, prm)
    x, _ = _ffn(x, prm["ffn2_norm"], prm["ffn2_w_in"], prm["ffn2_w_out"], prm["ffn2_norm"], False)
    return x, s_out, xo


def kernel(x_prompt, x_sample, cache_k, cache_v, cache_logf, page_table, state_rwkv, state_shift, state_s5_re, state_s5_im, meta_tokens, ffn1_norm, ffn1_w_in, ffn1_w_out, mix_norm, w_in, fox_b_f, fox_q_gain, fox_k_gain, rwkv_mu, rwkv_w0, rwkv_w2, rwkv_a0, rwkv_a2, rwkv_g2, rwkv_k_k, rwkv_k_a, rwkv_r_k, rwkv_lnx_w, rwkv_lnx_b, s5_a_re, s5_a_im, s5_log_dt, s5_b_re, s5_b_im, s5_c_re, s5_c_im, s5_d, s5_glu_w, s5_glu_b, w_branch, w_out, ffn2_norm, ffn2_w_in, ffn2_w_out):
    p = dict(ffn1_norm=ffn1_norm, ffn1_w_in=ffn1_w_in, ffn1_w_out=ffn1_w_out, mix_norm=mix_norm,
             w_in=w_in, fox_b_f=fox_b_f, fox_q_gain=fox_q_gain, fox_k_gain=fox_k_gain,
             rwkv_mu=rwkv_mu, rwkv_w0=rwkv_w0, rwkv_w2=rwkv_w2, rwkv_a0=rwkv_a0, rwkv_a2=rwkv_a2,
             rwkv_g2=rwkv_g2, rwkv_k_k=rwkv_k_k, rwkv_k_a=rwkv_k_a, rwkv_r_k=rwkv_r_k,
             rwkv_lnx_w=rwkv_lnx_w, rwkv_lnx_b=rwkv_lnx_b, s5_a_re=s5_a_re, s5_a_im=s5_a_im,
             s5_log_dt=s5_log_dt, s5_b_re=s5_b_re, s5_b_im=s5_b_im, s5_c_re=s5_c_re, s5_c_im=s5_c_im,
             s5_d=s5_d, s5_glu_w=s5_glu_w, s5_glu_b=s5_glu_b, w_branch=w_branch, w_out=w_out,
             ffn2_norm=ffn2_norm, ffn2_w_in=ffn2_w_in, ffn2_w_out=ffn2_w_out)
    depth = w_in.shape[0]
    prms = [_layer_params(l, p) for l in range(depth)]

    bp, seq, _ = x_prompt.shape
    t_p = seq + N_META
    tp = -(-t_p // SEQ_ALIGN) * SEQ_ALIGN
    meta = jnp.broadcast_to(meta_tokens.astype(F32)[None], (bp, N_META, D_MODEL))
    xp = jnp.concatenate([meta, x_prompt], axis=1)
    xp = jnp.pad(xp, ((0, 0), (0, tp - t_p), (0, 0))).transpose(1, 0, 2).reshape(tp * bp, D_MODEL)
    z_shift = jnp.zeros((bp, RW_PAD), F32)
    z_s = jnp.zeros((bp, N_HEADS, HEAD_DIM, HEAD_DIM), F32)
    z_x = jnp.zeros((bp, 2 * S5_LANES), F32)
    tm3 = lambda a, w: a.reshape(tp, bp, w)
    pk, pv, plf, prw, psh, pre, pim = [], [], [], [], [], [], []
    for l in range(depth):
        prm = prms[l]
        xp, qkvu, rwp, gates, qn, kn, lf, lfp = _mixer_inputs(xp, prm)
        oa = _fox_attention(qn, kn, qkvu, lfp, bp, tp)
        xp, s_out, xo = _layer_tail(xp, oa, qkvu, rwp, gates, prm, z_shift, z_s, z_x, bp, tp, t_p,
                                    RWKV_CHUNK, S5_CHUNK)
        pk.append(tm3(kn, MIX)[:t_p].transpose(1, 0, 2).reshape(bp, t_p, N_HEADS, HEAD_DIM))
        pv.append(tm3(qkvu, QKVU_COLS)[:t_p, :, 2 * MIX:3 * MIX].transpose(1, 0, 2)
                  .reshape(bp, t_p, N_HEADS, HEAD_DIM))
        plf.append(tm3(lf, LANE)[:t_p, :, :N_HEADS].transpose(1, 0, 2))
        prw.append(s_out)
        psh.append(tm3(rwp, RW_PAD)[t_p - 1, :, :RW_COLS])
        pre.append(xo[:, :S5_LANES].reshape(bp, S5_GROUPS, S5_STATE))
        pim.append(xo[:, S5_LANES:].reshape(bp, S5_GROUPS, S5_STATE))
    y_prompt = tm3(xp, D_MODEL)[N_META:t_p].transpose(1, 0, 2)

    bs = x_sample.shape[0]
    ts = 8
    xs = jnp.pad(x_sample[:, 0], ((0, (ts - 1) * bs), (0, 0)))
    n_phys = cache_k.shape[1]
    ck = cache_k.reshape(depth, n_phys, PAGE, MIX)
    cv = cache_v.reshape(depth, n_phys, PAGE, MIX)
    clf = cache_logf.transpose(0, 1, 3, 2)
    sk, sv, slf, srw, ssh, sre, sim = [], [], [], [], [], [], []
    for l in range(depth):
        prm = prms[l]
        xs, qkvu, rwp, gates, qn, kn, lf, lfp = _mixer_inputs(xs, prm)
        v_new = qkvu[:bs, 2 * MIX:3 * MIX]
        oa = _fox_decode(l, qn[:bs].astype(F32), kn[:bs], v_new, lf[:bs], ck, cv, clf, page_table)
        oa = jnp.pad(oa, ((0, (ts - 1) * bs), (0, 0)))
        shift0 = jnp.pad(state_shift[l].astype(F32), ((0, 0), (0, RW_PAD - RW_COLS)))
        x0 = jnp.concatenate([state_s5_re[l].reshape(bs, S5_LANES), state_s5_im[l].reshape(bs, S5_LANES)],
                             axis=1).astype(F32)
        xs, s_out, xo = _layer_tail(xs, oa, qkvu, rwp, gates, prm, shift0, state_rwkv[l].astype(F32), x0,
                                    bs, ts, 1, ts, ts)
        sk.append(kn[:bs].reshape(bs, 1, N_HEADS, HEAD_DIM))
        sv.append(v_new.reshape(bs, 1, N_HEADS, HEAD_DIM))
        slf.append(lf[:bs, :N_HEADS].reshape(bs, 1, N_HEADS))
        srw.append(s_out)
        ssh.append(rwp[:bs, :RW_COLS])
        sre.append(xo[:, :S5_LANES].reshape(bs, S5_GROUPS, S5_STATE))
        sim.append(xo[:, S5_LANES:].reshape(bs, S5_GROUPS, S5_STATE))
    y_sample = xs[:bs].reshape(bs, 1, D_MODEL)

    st = lambda xs_: jnp.stack(xs_, axis=0)
    return (y_prompt, y_sample, st(pk), st(pv), st(plf), st(prw), st(psh), st(pre), st(pim),
            st(sk), st(sv), st(slf), st(srw), st(ssh), st(sre), st(sim))
```

```python
import functools

import jax
import jax.numpy as jnp
from jax import lax
from jax.experimental import pallas as pl
from jax.experimental.pallas import tpu as pltpu

F32 = jnp.float32
BF16 = jnp.bfloat16

D_MODEL = 1024
N_META = 16
HEAD_DIM = 64
MIX = D_MODEL // 2
N_HEADS = MIX // HEAD_DIM
D_FF = 2816
RW_COLS = 1824
RW_PAD = 1920
LORA_WA = 64
LORA_G = 160
S5_GROUPS = 32
S5_GROUP = 16
S5_STATE = 64
S5_LANES = S5_GROUPS * S5_STATE
QKVU_COLS = 4 * MIX + 128
PAGE = 128
NORM_EPS = 1e-6
LNX_EPS = 64e-5

LANE = 128
SEQ_ALIGN = 128
FF_TILE = 256
ATT_Q_TILE = 512
RWKV_CHUNK = 64
S5_CHUNK = 64
S5_SLAB_PAD = 8
SUBLANE = 8
DECODE_PAGES_PER_STEP = 8
VMEM_LIMIT = 56 * 1024 * 1024

NN = (((1,), (0,)), ((), ()))
NT = (((1,), (1,)), ((), ()))
TN = (((0,), (0,)), ((), ()))


def _params(sem):
    return pltpu.CompilerParams(dimension_semantics=sem, vmem_limit_bytes=VMEM_LIMIT)


def _tile(n, target):
    best = None
    for t in range(8, min(n, target) + 1, 8):
        if n % t == 0:
            best = t
    return best or n


def _dot(a, b, dn=NN):
    return lax.dot_general(a, b, dn, preferred_element_type=F32)


def _split2(x):
    hi = x.astype(BF16)
    return hi, (x - hi.astype(F32)).astype(BF16)


def _dot3(a, b, dn=NN):
    ah, al = _split2(a)
    bh, bl = _split2(b)
    return _dot(ah, bh, dn) + _dot(ah, bl, dn) + _dot(al, bh, dn)


def _dot_exact_lhs(a01, b):
    a = a01.astype(BF16)
    b1 = b.astype(BF16)
    r1 = b - b1.astype(F32)
    b2 = r1.astype(BF16)
    b3 = (r1 - b2.astype(F32)).astype(BF16)
    return _dot(a, b1) + _dot(a, b2) + _dot(a, b3)


def _dot_exact_rhs(a, b01):
    b = b01.astype(BF16)
    a1 = a.astype(BF16)
    r1 = a - a1.astype(F32)
    a2 = r1.astype(BF16)
    a3 = (r1 - a2.astype(F32)).astype(BF16)
    return _dot(a1, b) + _dot(a2, b) + _dot(a3, b)


def _segsum(x, ones_bd):
    hi, lo = _split2(x)
    return _dot(hi, ones_bd) + _dot(lo, ones_bd)


def _softplus(x):
    return jnp.maximum(x, 0.0) + jnp.log1p(jnp.exp(-jnp.abs(x)))


def _rms(x, g):
    ms = jnp.mean(x * x, axis=-1, keepdims=True)
    return x * lax.rsqrt(ms + NORM_EPS) * g


def _ffn_kernel(x_ref, g_ref, wg_ref, wu_ref, wo_ref, gn_ref, *rest, emit_hn):
    if emit_hn:
        o_ref, hn_ref, xn_s, acc_s = rest
    else:
        o_ref, xn_s, acc_s = rest
    j = pl.program_id(1)

    @pl.when(j == 0)
    def _():
        xn_s[...] = _rms(x_ref[...], g_ref[...]).astype(BF16)
        acc_s[...] = jnp.zeros_like(acc_s)

    xn = xn_s[...]
    gate = _dot(xn, wg_ref[...])
    up = _dot(xn, wu_ref[...])
    act = (gate * jax.nn.sigmoid(gate) * up).astype(BF16)
    acc_s[...] += _dot(act, wo_ref[...])

    @pl.when(j == pl.num_programs(1) - 1)
    def _():
        y = x_ref[...] + 0.5 * acc_s[...]
        o_ref[...] = y
        if emit_hn:
            hn_ref[...] = _rms(y, gn_ref[...]).astype(BF16)


def _ffn(x, g, w_in, w_out, g_next, emit_hn):
    rows = x.shape[0]
    tm = _tile(rows, 1024)
    nf = D_FF // FF_TILE
    out_shape = [jax.ShapeDtypeStruct((rows, D_MODEL), F32)]
    out_specs = [pl.BlockSpec((tm, D_MODEL), lambda i, j: (i, 0))]
    if emit_hn:
        out_shape.append(jax.ShapeDtypeStruct((rows, D_MODEL), BF16))
        out_specs.append(pl.BlockSpec((tm, D_MODEL), lambda i, j: (i, 0)))
    res = pl.pallas_call(
        functools.partial(_ffn_kernel, emit_hn=emit_hn),
        grid=(rows // tm, nf),
        in_specs=[
            pl.BlockSpec((tm, D_MODEL), lambda i, j: (i, 0)),
            pl.BlockSpec((1, D_MODEL), lambda i, j: (0, 0)),
            pl.BlockSpec((D_MODEL, FF_TILE), lambda i, j: (0, j)),
            pl.BlockSpec((D_MODEL, FF_TILE), lambda i, j: (0, j + nf)),
            pl.BlockSpec((FF_TILE, D_MODEL), lambda i, j: (j, 0)),
            pl.BlockSpec((1, D_MODEL), lambda i, j: (0, 0)),
        ],
        out_specs=out_specs,
        out_shape=out_shape,
        scratch_shapes=[pltpu.VMEM((tm, D_MODEL), BF16), pltpu.VMEM((tm, D_MODEL), F32)],
        compiler_params=_params(("parallel", "arbitrary")),
    )(x, g, w_in, w_in, w_out, g_next)
    return res if emit_hn else (res[0], None)


def _matmul_kernel(a_ref, b_ref, o_ref):
    o_ref[...] = _dot(a_ref[...], b_ref[...])


def _matmul(a, b, tn):
    rows, k = a.shape
    n = b.shape[1]
    tm = _tile(rows, 1024)
    return pl.pallas_call(
        _matmul_kernel,
        grid=(rows // tm, n // tn),
        in_specs=[pl.BlockSpec((tm, k), lambda i, j: (i, 0)), pl.BlockSpec((k, tn), lambda i, j: (0, j))],
        out_specs=pl.BlockSpec((tm, tn), lambda i, j: (i, j)),
        out_shape=jax.ShapeDtypeStruct((rows, n), F32),
        compiler_params=_params(("parallel", "arbitrary")),
    )(a, b)


def _qk_prep_kernel(q_ref, k_ref, f_ref, qg_ref, kg_ref, bf_ref, ones_ref, qn_ref, kn_ref, lf_ref, lfp_ref):
    ones_bd = ones_ref[...]

    def headnorm(x, g):
        ms = _segsum(x * x, ones_bd) * (1.0 / HEAD_DIM)
        return x * lax.rsqrt(ms + NORM_EPS) * g

    qn_ref[...] = (headnorm(q_ref[...], qg_ref[...]) * (HEAD_DIM ** -0.5)).astype(BF16)
    kn_ref[...] = headnorm(k_ref[...], kg_ref[...])
    lf = -_softplus(-(f_ref[...] + bf_ref[...]))
    lf_ref[...] = lf
    lane = lax.broadcasted_iota(jnp.int32, lf.shape, 1)
    for hp in range(N_HEADS // 2):
        a = lf[:, 2 * hp:2 * hp + 1]
        b = lf[:, 2 * hp + 1:2 * hp + 2]
        lfp_ref[:, hp * LANE:(hp + 1) * LANE] = jnp.where(lane == 0, a, jnp.where(lane == 1, b, 0.0))


def _qk_prep(qkvu, qg, kg, bf, ones_bd):
    rows = qkvu.shape[0]
    tm = _tile(rows, 1024)
    small = lambda w: pl.BlockSpec((1, w), lambda i: (0, 0))
    return pl.pallas_call(
        _qk_prep_kernel,
        grid=(rows // tm,),
        in_specs=[
            pl.BlockSpec((tm, MIX), lambda i: (i, 0)),
            pl.BlockSpec((tm, MIX), lambda i: (i, 1)),
            pl.BlockSpec((tm, LANE), lambda i: (i, 4 * MIX // LANE)),
            small(MIX), small(MIX), small(LANE),
            pl.BlockSpec((MIX, MIX), lambda i: (0, 0)),
        ],
        out_specs=[
            pl.BlockSpec((tm, MIX), lambda i: (i, 0)),
            pl.BlockSpec((tm, MIX), lambda i: (i, 0)),
            pl.BlockSpec((tm, LANE), lambda i: (i, 0)),
            pl.BlockSpec((tm, MIX), lambda i: (i, 0)),
        ],
        out_shape=[
            jax.ShapeDtypeStruct((rows, MIX), BF16),
            jax.ShapeDtypeStruct((rows, MIX), F32),
            jax.ShapeDtypeStruct((rows, LANE), F32),
            jax.ShapeDtypeStruct((rows, MIX), F32),
        ],
        compiler_params=_params(("parallel",)),
    )(qkvu, qkvu, qkvu, qg, kg, bf, ones_bd)


def _fox_attn_kernel(q_ref, k_ref, v_ref, lf_ref, o_ref, c_s):
    tp = q_ref.shape[0]
    cb = LANE
    row = lax.broadcasted_iota(jnp.int32, (cb, cb), 0)
    col = lax.broadcasted_iota(jnp.int32, (cb, cb), 1)
    tri = (row >= col).astype(F32)
    carry = jnp.zeros((1, LANE), F32)
    for blk in range(tp // cb):
        cblk = _dot_exact_lhs(tri, lf_ref[blk * cb:(blk + 1) * cb, :]) + carry
        c_s[blk * cb:(blk + 1) * cb, :] = cblk
        carry = cblk[cb - 1:cb, :]
    c = c_s[...]
    c_t = c.T
    q = q_ref[...]
    k = k_ref[...].astype(BF16)
    v = v_ref[...].astype(BF16)
    lane = lax.broadcasted_iota(jnp.int32, (1, LANE), 1)
    for r0 in range(0, tp, ATT_Q_TILE):
        r1 = min(r0 + ATT_Q_TILE, tp)
        tq = r1 - r0
        qi = lax.broadcasted_iota(jnp.int32, (tq, r1), 0) + r0
        ki = lax.broadcasted_iota(jnp.int32, (tq, r1), 1)
        causal = ki <= qi
        outs = []
        for e in range(2):
            head = (lane // HEAD_DIM) == e
            qe = jnp.where(head, q[r0:r1, :], jnp.zeros((), BF16))
            s = _dot(qe, k[:r1, :], NT)
            s = s + c[r0:r1, e:e + 1] - c_t[e:e + 1, :r1]
            s = jnp.where(causal, s, -1e30)
            m = jnp.max(s, axis=-1, keepdims=True)
            p = jnp.exp(s - m)
            l = jnp.sum(p, axis=-1, keepdims=True)
            outs.append(_dot(p.astype(BF16), v[:r1, :]) / l)
        o_ref[r0:r1, :] = jnp.where((lane // HEAD_DIM) == 0, outs[0], outs[1])


def _fox_attention(qn, kn, qkvu, lfp, bsz, tp):
    hp = N_HEADS // 2
    out = pl.pallas_call(
        _fox_attn_kernel,
        grid=(bsz, hp),
        in_specs=[
            pl.BlockSpec((None, tp, LANE), lambda b, h: (b, 0, h)),
            pl.BlockSpec((None, tp, LANE), lambda b, h: (b, 0, h)),
            pl.BlockSpec((None, tp, LANE), lambda b, h: (b, 0, 2 * MIX // LANE + h)),
            pl.BlockSpec((None, tp, LANE), lambda b, h: (b, 0, h)),
        ],
        out_specs=pl.BlockSpec((None, tp, LANE), lambda b, h: (b, 0, h)),
        out_shape=jax.ShapeDtypeStruct((bsz, tp, MIX), F32),
        scratch_shapes=[pltpu.VMEM((tp, LANE), F32)],
        compiler_params=_params(("parallel", "parallel")),
    )(qn.reshape(bsz, tp, MIX), kn.reshape(bsz, tp, MIX), qkvu.reshape(bsz, tp, QKVU_COLS),
      lfp.reshape(bsz, tp, MIX))
    return out.reshape(bsz * tp, MIX)


def _fox_decode_kernel(pt_ref, q_ref, kn_ref, vn_ref, lfn_ref, *rest, pages_per_step):
    del pt_ref
    g = pages_per_step
    k_refs, v_refs, lf_refs = rest[:g], rest[g:2 * g], rest[2 * g:3 * g]
    o_ref, m_s, l_s, acc_s, carry_s = rest[3 * g:]
    p = pl.program_id(1)
    q = q_ref[...]

    @pl.when(p == 0)
    def _():
        m_s[...] = jnp.sum(q * kn_ref[...], axis=1, keepdims=True)
        l_s[...] = jnp.ones_like(l_s)
        acc_s[...] = vn_ref[...]
        carry_s[...] = lfn_ref[...]

    rows = PAGE * N_HEADS
    spos = lax.broadcasted_iota(jnp.int32, (PAGE, rows), 0)
    jpos = lax.shift_right_logical(lax.broadcasted_iota(jnp.int32, (PAGE, rows), 1), 3)
    later = (spos > jpos).astype(F32)
    hrow = lax.broadcasted_iota(jnp.int32, (N_HEADS, rows), 0)
    hcol = lax.broadcasted_iota(jnp.int32, (N_HEADS, rows), 1) & (N_HEADS - 1)
    own = hrow == hcol
    qb = q.astype(BF16)
    carry = carry_s[...]
    logits = []
    for i in range(g):
        lf = lf_refs[i][...]
        s = _dot(qb, k_refs[i][...].astype(BF16), NT) + (carry + _dot_exact_rhs(lf, later))
        logits.append(jnp.where(own, s, -1e30))
        carry = carry + jnp.sum(lf, axis=1, keepdims=True)
    carry_s[...] = carry
    m_old = m_s[...]
    m_new = m_old
    for s in logits:
        m_new = jnp.maximum(m_new, jnp.max(s, axis=1, keepdims=True))
    alpha = jnp.exp(m_old - m_new)
    l = l_s[...] * alpha
    acc = acc_s[...] * alpha
    for i in range(g):
        pr = jnp.exp(logits[i] - m_new)
        l = l + jnp.sum(pr, axis=1, keepdims=True)
        acc = acc + _dot(pr.astype(BF16), v_refs[i][...].astype(BF16))
    m_s[...], l_s[...], acc_s[...] = m_new, l, acc

    @pl.when(p == pl.num_programs(1) - 1)
    def _():
        o_ref[...] = acc / l


def _fox_decode(layer, q, kn, vn, lfn, cache_k, cache_v, cache_lft, page_table):
    bsz, n_pages = page_table.shape
    g = DECODE_PAGES_PER_STEP
    assert n_pages % g == 0

    def page_spec(block, i):
        return pl.BlockSpec((None, None) + block,
                            lambda b, p, pt: (layer, pt[b, n_pages - 1 - (p * g + i)], 0, 0))

    tok = lambda w: pl.BlockSpec((None, N_HEADS, w), lambda b, p, pt: (b, 0, 0))
    in_specs = [tok(HEAD_DIM), tok(HEAD_DIM), tok(HEAD_DIM), tok(1)]
    in_specs += [page_spec((PAGE * N_HEADS, HEAD_DIM), i) for i in range(g)]
    in_specs += [page_spec((PAGE * N_HEADS, HEAD_DIM), i) for i in range(g)]
    in_specs += [page_spec((N_HEADS, PAGE), i) for i in range(g)]
    grid_spec = pltpu.PrefetchScalarGridSpec(
        num_scalar_prefetch=1,
        grid=(bsz, n_pages // g),
        in_specs=in_specs,
        out_specs=tok(HEAD_DIM),
        scratch_shapes=[pltpu.VMEM((N_HEADS, 1), F32), pltpu.VMEM((N_HEADS, 1), F32),
                        pltpu.VMEM((N_HEADS, HEAD_DIM), F32), pltpu.VMEM((N_HEADS, 1), F32)],
    )
    return pl.pallas_call(
        functools.partial(_fox_decode_kernel, pages_per_step=g),
        grid_spec=grid_spec,
        out_shape=jax.ShapeDtypeStruct((bsz, N_HEADS, HEAD_DIM), F32),
        compiler_params=_params(("parallel", "arbitrary")),
    )(page_table, q, kn, vn, lfn, *([cache_k] * g), *([cache_v] * g), *([cache_lft] * g))


def _rwkv_prep_kernel(rw_ref, prev_ref, sh0_ref, mu_ref, w0_ref, w2_ref, a0_ref, a2_ref, g2_ref,
                      kk_ref, ka_ref, rk_ref, ones_ref,
                      r_out, lw_out, k_out, v_out, kk_out, b_out, g_out, bonus_out):
    i = pl.program_id(1)
    rw = rw_ref[...]
    tm = rw.shape[0]
    first = jnp.where(i == 0, sh0_ref[...], prev_ref[SUBLANE - 1:SUBLANE, :])
    trow = lax.broadcasted_iota(jnp.int32, (tm, 1), 0)
    prev = jnp.where(trow == 0, first, pltpu.roll(rw, 1, 0))
    z = rw + (prev - rw) * mu_ref[...]
    r = z[:, 0:MIX]
    k = z[:, MIX:2 * MIX]
    v = z[:, 2 * MIX:3 * MIX]
    xwa = z[:, 3 * MIX:3 * MIX + 2 * LORA_WA]
    xg = z[:, 3 * MIX + 2 * LORA_WA:]
    ones_bd = ones_ref[...]
    w = -_softplus(-(w0_ref[...] + _dot(jnp.tanh(xwa).astype(BF16), w2_ref[...]))) - 0.5
    lw_out[...] = -jnp.exp(w)
    a = jax.nn.sigmoid(a0_ref[...] + _dot(xwa.astype(BF16), a2_ref[...]))
    g_out[...] = _dot(jax.nn.sigmoid(xg).astype(BF16), g2_ref[...])
    kk = k * kk_ref[...]
    kk = kk / jnp.maximum(jnp.sqrt(_segsum(kk * kk, ones_bd)), 1e-12)
    k2 = k * (1.0 + (a - 1.0) * ka_ref[...])
    r_out[...] = r
    k_out[...] = k2
    v_out[...] = v
    kk_out[...] = kk
    b_out[...] = kk * a
    bonus_out[...] = _segsum(r * k2 * rk_ref[...], ones_bd) * v


def _rwkv_prep(rwp, shift0, prm, bsz, tp):
    tm = _tile(tp, 544)
    nb = tm // SUBLANE
    small = lambda a: pl.BlockSpec(a.shape, lambda b, i: (0, 0))
    consts = [prm["mu"], prm["w0"], prm["w2"], prm["a0"], prm["a2"], prm["g2"], prm["k_k"], prm["k_a"],
              prm["r_k"], prm["ones_bd"]]
    rw3 = rwp.reshape(bsz, tp, RW_PAD)
    outs = pl.pallas_call(
        _rwkv_prep_kernel,
        grid=(bsz, tp // tm),
        in_specs=[
            pl.BlockSpec((None, tm, RW_PAD), lambda b, i: (b, i, 0)),
            pl.BlockSpec((None, SUBLANE, RW_PAD), lambda b, i: (b, jnp.maximum(i * nb - 1, 0), 0)),
            pl.BlockSpec((None, 1, RW_PAD), lambda b, i: (b, 0, 0)),
        ] + [small(c) for c in consts],
        out_specs=[pl.BlockSpec((None, tm, MIX), lambda b, i: (b, i, 0))] * 8,
        out_shape=[jax.ShapeDtypeStruct((bsz, tp, MIX), F32)] * 8,
        compiler_params=_params(("parallel", "parallel")),
    )(rw3, rw3, shift0, *consts)
    return outs


def _rwkv_scan_kernel(r_ref, lw_ref, k_ref, v_ref, kk_ref, b_ref, s0_ref, o_ref, s_out, s_s, *, chunk, t_valid):
    c = pl.program_id(1)
    n = chunk

    @pl.when(c == 0)
    def _():
        s_s[...] = s0_ref[...]

    trow = lax.broadcasted_iota(jnp.int32, (n, 1), 0) + c * n
    valid = trow < t_valid
    lw = jnp.where(valid, lw_ref[...], 0.0)
    kk = jnp.where(valid, kk_ref[...], 0.0)
    bb = jnp.where(valid, b_ref[...], 0.0)
    kx = jnp.where(valid, k_ref[...], 0.0)
    vx = jnp.where(valid, v_ref[...], 0.0)
    rx = r_ref[...]

    row = lax.broadcasted_iota(jnp.int32, (n, n), 0)
    col = lax.broadcasted_iota(jnp.int32, (n, n), 1)
    incl = row >= col
    strict = row > col
    eye = (row == col).astype(F32)
    ci = _dot_exact_lhs(incl.astype(F32), lw)
    ce = ci - lw
    kkd = kk * jnp.exp(ce)
    rd = rx * jnp.exp(ci)
    einv = jnp.exp(-ci)
    bd = bb * einv
    kd = kx * einv
    cl = ci[n - 1:n, :]
    wl = jnp.exp(cl)
    dl = jnp.exp(cl - ci)
    bdw = bb * dl
    kdw = kx * dl
    row2 = lax.broadcasted_iota(jnp.int32, (n, 2 * n), 0)
    col2 = lax.broadcasted_iota(jnp.int32, (n, 2 * n), 1)
    col2 = jnp.where(col2 >= n, col2 - n, col2)
    strict2 = row2 > col2
    incl2 = row2 >= col2

    hs = range(N_HEADS)
    sls = [slice(h * HEAD_DIM, (h + 1) * HEAD_DIM) for h in hs]
    s_all = [s_s[h] for h in hs]
    a_m = [jnp.concatenate([kkd[:, sl], rd[:, sl]], axis=0) for sl in sls]
    b_m = [jnp.concatenate([bd[:, sl], kd[:, sl]], axis=0) for sl in sls]
    pm = [_dot3(a_m[h], b_m[h], NT) for h in hs]
    p_top = [jnp.where(strict2, pm[h][:n, :], 0.0) for h in hs]
    p_bot = [jnp.where(incl2, pm[h][n:, :], 0.0) for h in hs]
    gm = [_dot3(a_m[h], s_all[h], NT) for h in hs]
    x = [-p_top[h][:, :n] for h in hs]
    tinv = [eye + x[h] for h in hs]
    span = 2
    while span < n:
        x = [_dot3(x[h], x[h]) for h in hs]
        tinv = [tinv[h] + _dot3(tinv[h], x[h]) for h in hs]
        span *= 2
    v_h = [vx[:, sl] for sl in sls]
    rhs = [gm[h][:n, :] + _dot3(p_top[h][:, n:], v_h[h]) for h in hs]
    u = [-_dot3(tinv[h], rhs[h]) for h in hs]
    uv = [jnp.concatenate([u[h], v_h[h]], axis=0) for h in hs]
    outs = [gm[h][n:, :] + _dot3(p_bot[h], uv[h]) for h in hs]
    bk = [jnp.concatenate([bdw[:, sl], kdw[:, sl]], axis=0) for sl in sls]
    s_new = [s_all[h] * wl[:, sls[h]] + _dot3(uv[h], bk[h], TN) for h in hs]
    for h in hs:
        s_s[h] = s_new[h]
    o_ref[...] = jnp.concatenate(outs, axis=1)

    @pl.when(c == pl.num_programs(1) - 1)
    def _():
        s_out[...] = s_s[...]


def _rwkv_scan(r, lw, k, v, kk, b, s0, bsz, tp, t_valid, chunk):
    blk = pl.BlockSpec((None, chunk, MIX), lambda bi, c: (bi, c, 0))
    st = pl.BlockSpec((None, N_HEADS, HEAD_DIM, HEAD_DIM), lambda bi, c: (bi, 0, 0, 0))
    o, s_out = pl.pallas_call(
        functools.partial(_rwkv_scan_kernel, chunk=chunk, t_valid=t_valid),
        grid=(bsz, tp // chunk),
        in_specs=[blk] * 6 + [st],
        out_specs=[blk, st],
        out_shape=[jax.ShapeDtypeStruct((bsz, tp, MIX), F32),
                   jax.ShapeDtypeStruct((bsz, N_HEADS, HEAD_DIM, HEAD_DIM), F32)],
        scratch_shapes=[pltpu.VMEM((N_HEADS, HEAD_DIM, HEAD_DIM), F32)],
        compiler_params=_params(("parallel", "arbitrary")),
    )(r, lw, k, v, kk, b, s0)
    return o.reshape(bsz * tp, MIX), s_out


def _s5_kernel(u_ref, x0_ref, a_ref, bm_ref, cm_ref, d_ref, gw_ref, gb_ref, o_ref, xo_ref, buf_s, st_s,
               *, t_valid):
    c = pl.program_id(0)
    bsz, tc, _ = u_ref.shape
    tcp = buf_s.shape[1] // bsz
    half = S5_LANES // 2
    hc = MIX // 2
    nblk = S5_LANES // LANE
    hblk = half // LANE

    @pl.when(c == 0)
    def _():
        st_s[...] = x0_ref[...]

    u2 = u_ref[...].reshape(bsz * tc, MIX)
    ub = u2.astype(BF16)
    for part in range(2):
        for j in range(2):
            col0 = part * S5_LANES + j * half
            res = _dot(ub[:, j * hc:(j + 1) * hc], bm_ref[j * hc:(j + 1) * hc, col0:col0 + half])
            for k in range(hblk):
                for b in range(bsz):
                    buf_s[col0 // LANE + k, b * tcp:b * tcp + tc, :] = res[b * tc:(b + 1) * tc,
                                                                           k * LANE:(k + 1) * LANE]

    gb = max(1, 2 * nblk // bsz)
    for g0 in range(0, nblk, gb):
        blks = list(range(g0, g0 + gb))
        lane = lambda k: slice(k * LANE, (k + 1) * LANE)
        ar = [jnp.broadcast_to(a_ref[:, lane(k)], (bsz, LANE)) for k in blks]
        ai = [jnp.broadcast_to(a_ref[:, lane(nblk + k)], (bsz, LANE)) for k in blks]

        def body(t, carry, blks=blks, ar=ar, ai=ai):
            rows = pl.ds(t, bsz, stride=tcp)
            out = []
            for n, k in enumerate(blks):
                xr, xi = carry[n]
                nxr = ar[n] * xr - ai[n] * xi + buf_s[k, rows, :]
                nxi = ar[n] * xi + ai[n] * xr + buf_s[nblk + k, rows, :]
                buf_s[k, rows, :] = nxr
                buf_s[nblk + k, rows, :] = nxi
                out.append((nxr, nxi))
            return tuple(out)

        init = tuple((st_s[:, lane(k)], st_s[:, lane(nblk + k)]) for k in blks)
        fin = lax.fori_loop(0, tc, body, init)
        for n, k in enumerate(blks):
            st_s[:, lane(k)] = fin[n][0]
            st_s[:, lane(nblk + k)] = fin[n][1]

    c_valid, t_in = divmod(t_valid - 1, tc)

    @pl.when(c == c_valid)
    def _():
        for k in range(2 * nblk):
            xo_ref[:, k * LANE:(k + 1) * LANE] = buf_s[k, pl.ds(t_in, bsz, stride=tcp), :]

    ys = []
    for j in range(2):
        acc = None
        for part in range(2):
            col0 = part * S5_LANES + j * half
            xs = jnp.concatenate(
                [jnp.concatenate([buf_s[col0 // LANE + k, b * tcp:b * tcp + tc, :] for b in range(bsz)], axis=0)
                 for k in range(hblk)], axis=1)
            d = _dot(xs.astype(BF16), cm_ref[col0:col0 + half, j * hc:(j + 1) * hc])
            acc = d if acc is None else acc + d
        ys.append(acc)
    y = jnp.concatenate(ys, axis=1) + d_ref[...] * u2
    y = 0.5 * y * (1.0 + jnp.tanh(0.7978845608028654 * (y + 0.044715 * (y * y * y))))
    gate = jax.nn.sigmoid(_dot(y.astype(BF16), gw_ref[...]) + gb_ref[...])
    o_ref[...] = (y * gate).reshape(bsz, tc, MIX)


def _s5(qkvu, x0, prm, bsz, tp, t_valid, tc):
    n2 = 2 * S5_LANES
    full = lambda a: pl.BlockSpec(a.shape, lambda c: (0,) * a.ndim)
    consts = [prm["s5_a"], prm["s5_bm"], prm["s5_cm"], prm["s5_d"], prm["s5_gw"], prm["s5_gb"]]
    o, xo = pl.pallas_call(
        functools.partial(_s5_kernel, t_valid=t_valid),
        grid=(tp // tc,),
        in_specs=[pl.BlockSpec((bsz, tc, MIX), lambda c: (0, c, 3)), full(x0)] + [full(a) for a in consts],
        out_specs=[pl.BlockSpec((bsz, tc, MIX), lambda c: (0, c, 0)), pl.BlockSpec((bsz, n2), lambda c: (0, 0))],
        out_shape=[jax.ShapeDtypeStruct((bsz, tp, MIX), F32), jax.ShapeDtypeStruct((bsz, n2), F32)],
        scratch_shapes=[pltpu.VMEM((n2 // LANE, bsz * (tc + S5_SLAB_PAD), LANE), F32),
                        pltpu.VMEM((bsz, n2), F32)],
        compiler_params=_params(("arbitrary",)),
    )(qkvu.reshape(bsz, tp, QKVU_COLS), x0, *consts)
    return o.reshape(bsz * tp, MIX), xo


def _merge_kernel(x_ref, oa_ref, ob_ref, g_ref, bonus_ref, oc_ref, gates_ref, lw_ref, lb_ref, ones_ref,
                  wb_ref, wo_ref, o_ref):
    ones_bd = ones_ref[...]
    o = ob_ref[...]
    mu = _segsum(o, ones_bd) * (1.0 / HEAD_DIM)
    dlt = o - mu
    var = _segsum(dlt * dlt, ones_bd) * (1.0 / HEAD_DIM)
    ob = (dlt * lax.rsqrt(var + LNX_EPS) * lw_ref[...] + lb_ref[...] + bonus_ref[...]) * g_ref[...]
    merged = jax.nn.sigmoid(gates_ref[:, 0:D_MODEL]) * _dot(oa_ref[...].astype(BF16), wb_ref[0:MIX, :])
    merged += jax.nn.sigmoid(gates_ref[:, D_MODEL:2 * D_MODEL]) * _dot(ob.astype(BF16), wb_ref[MIX:2 * MIX, :])
    merged += jax.nn.sigmoid(gates_ref[:, 2 * D_MODEL:]) * _dot(oc_ref[...].astype(BF16), wb_ref[2 * MIX:, :])
    o_ref[...] = x_ref[...] + _dot(merged.astype(BF16), wo_ref[...])


def _merge(x, oa, ob, g, bonus, oc, gates, prm):
    rows = x.shape[0]
    tm = _tile(rows, 512)
    row = lambda w: pl.BlockSpec((tm, w), lambda i: (i, 0))
    full = lambda a: pl.BlockSpec(a.shape, lambda i: (0, 0))
    consts = [prm["lnx_w"], prm["lnx_b"], prm["ones_bd"], prm["w_branch"], prm["w_out"]]
    return pl.pallas_call(
        _merge_kernel,
        grid=(rows // tm,),
        in_specs=[row(D_MODEL), row(MIX), row(MIX), row(MIX), row(MIX), row(MIX), row(3 * D_MODEL)]
        + [full(c) for c in consts],
        out_specs=row(D_MODEL),
        out_shape=jax.ShapeDtypeStruct((rows, D_MODEL), F32),
        compiler_params=_params(("parallel",)),
    )(x, oa, ob, g, bonus, oc, gates, *consts)


def _layer_params(l, p):
    bf = lambda a: a.astype(BF16)
    row = lambda a, w=None: (a if w is None else jnp.pad(a, (0, w - a.shape[0]))).reshape(1, -1).astype(F32)
    w_in = p["w_in"][l]
    o_f = 3 * MIX
    o_rw = o_f + N_HEADS
    o_u = o_rw + RW_COLS
    o_g = o_u + MIX
    w_qkvu = jnp.concatenate(
        [w_in[:, :o_f], w_in[:, o_u:o_g], jnp.pad(w_in[:, o_f:o_rw], ((0, 0), (0, LANE - N_HEADS)))], axis=1)
    w_rw = jnp.pad(w_in[:, o_rw:o_u], ((0, 0), (0, RW_PAD - RW_COLS)))
    w_gates = w_in[:, o_g:]

    ar = p["s5_a_re"][l].astype(F32)
    ai = p["s5_a_im"][l].astype(F32)
    dt = jnp.exp(p["s5_log_dt"][l].astype(F32))[:, None]
    mag = jnp.exp(dt * ar)
    abar_re = mag * jnp.cos(dt * ai)
    abar_im = mag * jnp.sin(dt * ai)
    den = ar * ar + ai * ai
    nr, ni = abar_re - 1.0, abar_im
    z_re = (nr * ar + ni * ai) / den
    z_im = (ni * ar - nr * ai) / den
    b_re = p["s5_b_re"][l].astype(F32)
    b_im = p["s5_b_im"][l].astype(F32)
    bb_re = z_re[..., None] * b_re - z_im[..., None] * b_im
    bb_im = z_re[..., None] * b_im + z_im[..., None] * b_re
    eye = jnp.eye(S5_GROUPS, dtype=F32)
    blk_in = lambda m: jnp.einsum("gpc,gh->gchp", m, eye).reshape(MIX, S5_LANES)
    blk_out = lambda m: jnp.einsum("gcp,gh->gphc", m, eye).reshape(S5_LANES, MIX)
    s5_bm = jnp.concatenate([blk_in(bb_re), blk_in(bb_im)], axis=1)
    s5_cm = jnp.concatenate([blk_out(p["s5_c_re"][l].astype(F32)), -blk_out(p["s5_c_im"][l].astype(F32))], axis=0)

    hd = lax.broadcasted_iota(jnp.int32, (MIX, MIX), 0) // HEAD_DIM
    hd_t = lax.broadcasted_iota(jnp.int32, (MIX, MIX), 1) // HEAD_DIM
    return dict(
        ffn1_norm=row(p["ffn1_norm"][l]), ffn1_w_in=bf(p["ffn1_w_in"][l]), ffn1_w_out=bf(p["ffn1_w_out"][l]),
        ffn2_norm=row(p["ffn2_norm"][l]), ffn2_w_in=bf(p["ffn2_w_in"][l]), ffn2_w_out=bf(p["ffn2_w_out"][l]),
        mix_norm=row(p["mix_norm"][l]),
        w_qkvu=bf(w_qkvu), w_rw=bf(w_rw), w_gates=bf(w_gates),
        q_gain=row(jnp.tile(p["fox_q_gain"][l], N_HEADS)), k_gain=row(jnp.tile(p["fox_k_gain"][l], N_HEADS)),
        b_f=row(p["fox_b_f"][l], LANE),
        ones_bd=(hd == hd_t).astype(BF16),
        mu=row(p["rwkv_mu"][l], RW_PAD), w0=row(p["rwkv_w0"][l]), a0=row(p["rwkv_a0"][l]),
        w2=bf(jnp.pad(p["rwkv_w2"][l], ((0, LORA_WA), (0, 0)))),
        a2=bf(jnp.pad(p["rwkv_a2"][l], ((LORA_WA, 0), (0, 0)))),
        g2=bf(jnp.pad(p["rwkv_g2"][l], ((0, RW_PAD - RW_COLS), (0, 0)))),
        k_k=row(p["rwkv_k_k"][l]), k_a=row(p["rwkv_k_a"][l]), r_k=row(p["rwkv_r_k"][l].reshape(-1)),
        lnx_w=row(p["rwkv_lnx_w"][l]), lnx_b=row(p["rwkv_lnx_b"][l]),
        s5_a=jnp.concatenate([abar_re.reshape(1, -1), abar_im.reshape(1, -1)], axis=1),
        s5_bm=bf(s5_bm), s5_cm=bf(s5_cm), s5_d=row(p["s5_d"][l].reshape(-1)),
        s5_gw=bf(p["s5_glu_w"][l]), s5_gb=row(p["s5_glu_b"][l]),
        w_branch=bf(p["w_branch"][l]), w_out=bf(p["w_out"][l]),
    )


def _mixer_inputs(x, prm):
    x, hn = _ffn(x, prm["ffn1_norm"], prm["ffn1_w_in"], prm["ffn1_w_out"], prm["mix_norm"], True)
    qkvu = _matmul(hn, prm["w_qkvu"], QKVU_COLS)
    rwp = _matmul(hn, prm["w_rw"], RW_PAD)
    gates = _matmul(hn, prm["w_gates"], D_MODEL)
    qn, kn, lf, lfp = _qk_prep(qkvu, prm["q_gain"], prm["k_gain"], prm["b_f"], prm["ones_bd"])
    return x, qkvu, rwp, gates, qn, kn, lf, lfp


def _layer_tail(x, oa, qkvu, rwp, gates, prm, shift0, s0, x0, bsz, tp, t_valid, chunk, tc):
    r, lw, k2, v, kk, b, g, bonus = _rwkv_prep(rwp, shift0, prm, bsz, tp)
    ob, s_out = _rwkv_scan(r, lw, k2, v, kk, b, s0, bsz, tp, t_valid, chunk)
    oc, xo = _s5(qkvu, x0, prm, bsz, tp, t_valid, tc)
    flat = lambda a: a.reshape(bsz * tp, MIX)
    x = _merge(x, oa, ob, flat(g), flat(bonus), oc, gates, prm)
    x, _ = _ffn(x, prm["ffn2_norm"], prm["ffn2_w_in"], prm["ffn2_w_out"], prm["ffn2_norm"], False)
    return x, s_out, xo


def kernel(x_prompt, x_sample, cache_k, cache_v, cache_logf, page_table, state_rwkv, state_shift, state_s5_re, state_s5_im, meta_tokens, ffn1_norm, ffn1_w_in, ffn1_w_out, mix_norm, w_in, fox_b_f, fox_q_gain, fox_k_gain, rwkv_mu, rwkv_w0, rwkv_w2, rwkv_a0, rwkv_a2, rwkv_g2, rwkv_k_k, rwkv_k_a, rwkv_r_k, rwkv_lnx_w, rwkv_lnx_b, s5_a_re, s5_a_im, s5_log_dt, s5_b_re, s5_b_im, s5_c_re, s5_c_im, s5_d, s5_glu_w, s5_glu_b, w_branch, w_out, ffn2_norm, ffn2_w_in, ffn2_w_out):
    p = dict(ffn1_norm=ffn1_norm, ffn1_w_in=ffn1_w_in, ffn1_w_out=ffn1_w_out, mix_norm=mix_norm,
             w_in=w_in, fox_b_f=fox_b_f, fox_q_gain=fox_q_gain, fox_k_gain=fox_k_gain,
             rwkv_mu=rwkv_mu, rwkv_w0=rwkv_w0, rwkv_w2=rwkv_w2, rwkv_a0=rwkv_a0, rwkv_a2=rwkv_a2,
             rwkv_g2=rwkv_g2, rwkv_k_k=rwkv_k_k, rwkv_k_a=rwkv_k_a, rwkv_r_k=rwkv_r_k,
             rwkv_lnx_w=rwkv_lnx_w, rwkv_lnx_b=rwkv_lnx_b, s5_a_re=s5_a_re, s5_a_im=s5_a_im,
             s5_log_dt=s5_log_dt, s5_b_re=s5_b_re, s5_b_im=s5_b_im, s5_c_re=s5_c_re, s5_c_im=s5_c_im,
             s5_d=s5_d, s5_glu_w=s5_glu_w, s5_glu_b=s5_glu_b, w_branch=w_branch, w_out=w_out,
             ffn2_norm=ffn2_norm, ffn2_w_in=ffn2_w_in, ffn2_w_out=ffn2_w_out)
    depth = w_in.shape[0]
    prms = [_layer_params(l, p) for l in range(depth)]

    bp, seq, _ = x_prompt.shape
    t_p = seq + N_META
    tp = -(-t_p // SEQ_ALIGN) * SEQ_ALIGN
    meta = jnp.broadcast_to(meta_tokens.astype(F32)[None], (bp, N_META, D_MODEL))
    xp = jnp.concatenate([meta, x_prompt], axis=1)
    xp = jnp.pad(xp, ((0, 0), (0, tp - t_p), (0, 0))).reshape(bp * tp, D_MODEL)
    z_shift = jnp.zeros((bp, 1, RW_PAD), F32)
    z_s = jnp.zeros((bp, N_HEADS, HEAD_DIM, HEAD_DIM), F32)
    z_x = jnp.zeros((bp, 2 * S5_LANES), F32)
    seqs = lambda a, w: a.reshape(bp, tp, w)
    pk, pv, plf, prw, psh, pre, pim = [], [], [], [], [], [], []
    for l in range(depth):
        prm = prms[l]
        xp, qkvu, rwp, gates, qn, kn, lf, lfp = _mixer_inputs(xp, prm)
        oa = _fox_attention(qn, kn, qkvu, lfp, bp, tp)
        xp, s_out, xo = _layer_tail(xp, oa, qkvu, rwp, gates, prm, z_shift, z_s, z_x, bp, tp, t_p,
                                    RWKV_CHUNK, S5_CHUNK)
        pk.append(seqs(kn, MIX)[:, :t_p].reshape(bp, t_p, N_HEADS, HEAD_DIM))
        pv.append(seqs(qkvu, QKVU_COLS)[:, :t_p, 2 * MIX:3 * MIX].reshape(bp, t_p, N_HEADS, HEAD_DIM))
        plf.append(seqs(lf, LANE)[:, :t_p, :N_HEADS])
        prw.append(s_out)
        psh.append(seqs(rwp, RW_PAD)[:, t_p - 1, :RW_COLS])
        pre.append(xo[:, :S5_LANES].reshape(bp, S5_GROUPS, S5_STATE))
        pim.append(xo[:, S5_LANES:].reshape(bp, S5_GROUPS, S5_STATE))
    y_prompt = seqs(xp, D_MODEL)[:, N_META:t_p]

    bs = x_sample.shape[0]
    ts = SUBLANE
    xs = jnp.pad(x_sample, ((0, 0), (0, ts - 1), (0, 0))).reshape(bs * ts, D_MODEL)
    n_phys = cache_k.shape[1]
    ck = cache_k.reshape(depth, n_phys, PAGE * N_HEADS, HEAD_DIM)
    cv = cache_v.reshape(depth, n_phys, PAGE * N_HEADS, HEAD_DIM)
    clf = cache_logf.transpose(0, 1, 3, 2)
    tok0 = lambda a, w: a.reshape(bs, ts, w)[:, 0]
    heads = lambda a: a.reshape(bs, N_HEADS, HEAD_DIM)
    sk, sv, slf, srw, ssh, sre, sim = [], [], [], [], [], [], []
    for l in range(depth):
        prm = prms[l]
        xs, qkvu, rwp, gates, qn, kn, lf, lfp = _mixer_inputs(xs, prm)
        q_new = tok0(qn, MIX).astype(F32)
        k_new = tok0(kn, MIX)
        v_new = tok0(qkvu, QKVU_COLS)[:, 2 * MIX:3 * MIX]
        lf_new = tok0(lf, LANE)[:, :N_HEADS]
        oa = _fox_decode(l, heads(q_new), heads(k_new), heads(v_new), lf_new[:, :, None], ck, cv, clf, page_table)
        oa = jnp.pad(oa.reshape(bs, 1, MIX), ((0, 0), (0, ts - 1), (0, 0))).reshape(bs * ts, MIX)
        shift0 = jnp.pad(state_shift[l].astype(F32), ((0, 0), (0, RW_PAD - RW_COLS)))[:, None]
        x0 = jnp.concatenate([state_s5_re[l].reshape(bs, S5_LANES), state_s5_im[l].reshape(bs, S5_LANES)],
                             axis=1).astype(F32)
        xs, s_out, xo = _layer_tail(xs, oa, qkvu, rwp, gates, prm, shift0, state_rwkv[l].astype(F32), x0,
                                    bs, ts, 1, ts, ts)
        sk.append(k_new.reshape(bs, 1, N_HEADS, HEAD_DIM))
        sv.append(v_new.reshape(bs, 1, N_HEADS, HEAD_DIM))
        slf.append(lf_new.reshape(bs, 1, N_HEADS))
        srw.append(s_out)
        ssh.append(tok0(rwp, RW_PAD)[:, :RW_COLS])
        sre.append(xo[:, :S5_LANES].reshape(bs, S5_GROUPS, S5_STATE))
        sim.append(xo[:, S5_LANES:].reshape(bs, S5_GROUPS, S5_STATE))
    y_sample = tok0(xs, D_MODEL).reshape(bs, 1, D_MODEL)

    st = lambda xs_: jnp.stack(xs_, axis=0)
    return (y_prompt, y_sample, st(pk), st(pv), st(plf), st(prw), st(psh), st(pre), st(pim),
            st(sk), st(sv), st(slf), st(srw), st(ssh), st(sre), st(sim))
```

```python
import functools

import jax
import jax.numpy as jnp
from jax import lax
from jax.experimental import pallas as pl
from jax.experimental.pallas import tpu as pltpu

F32 = jnp.float32
BF16 = jnp.bfloat16

D_MODEL = 1024
N_META = 16
HEAD_DIM = 64
MIX = D_MODEL // 2
N_HEADS = MIX // HEAD_DIM
D_FF = 2816
RW_COLS = 1824
RW_PAD = 1920
LORA_WA = 64
LORA_G = 160
S5_GROUPS = 32
S5_GROUP = 16
S5_STATE = 64
S5_LANES = S5_GROUPS * S5_STATE
QKVU_COLS = 4 * MIX + 128
PAGE = 128
NORM_EPS = 1e-6
LNX_EPS = 64e-5

LANE = 128
SEQ_ALIGN = 128
FF_TILE = 256
ATT_Q_TILE = 512
RWKV_CHUNK = 64
S5_CHUNK = 64
S5_SLAB_PAD = 8
SUBLANE = 8
DECODE_PAGES_PER_STEP = 8
VMEM_LIMIT = 56 * 1024 * 1024

NN = (((1,), (0,)), ((), ()))
NT = (((1,), (1,)), ((), ()))
TN = (((0,), (0,)), ((), ()))


def _params(sem):
    return pltpu.CompilerParams(dimension_semantics=sem, vmem_limit_bytes=VMEM_LIMIT)


def _tile(n, target):
    best = None
    for t in range(8, min(n, target) + 1, 8):
        if n % t == 0:
            best = t
    return best or n


def _dot(a, b, dn=NN):
    return lax.dot_general(a, b, dn, preferred_element_type=F32)


def _split2(x):
    hi = x.astype(BF16)
    return hi, (x - hi.astype(F32)).astype(BF16)


def _dot1(a, b, dn=NN):
    return _dot(a.astype(BF16), b.astype(BF16), dn)


def _dot3(a, b, dn=NN):
    ah, al = _split2(a)
    bh, bl = _split2(b)
    return _dot(ah, bh, dn) + _dot(ah, bl, dn) + _dot(al, bh, dn)


def _dot_exact_lhs(a01, b):
    a = a01.astype(BF16)
    b1 = b.astype(BF16)
    r1 = b - b1.astype(F32)
    b2 = r1.astype(BF16)
    b3 = (r1 - b2.astype(F32)).astype(BF16)
    return _dot(a, b1) + _dot(a, b2) + _dot(a, b3)


def _dot_exact_rhs(a, b01):
    b = b01.astype(BF16)
    a1 = a.astype(BF16)
    r1 = a - a1.astype(F32)
    a2 = r1.astype(BF16)
    a3 = (r1 - a2.astype(F32)).astype(BF16)
    return _dot(a1, b) + _dot(a2, b) + _dot(a3, b)


def _segsum(x, ones_bd):
    hi, lo = _split2(x)
    return _dot(hi, ones_bd) + _dot(lo, ones_bd)


def _softplus(x):
    return jnp.maximum(x, 0.0) + jnp.log1p(jnp.exp(-jnp.abs(x)))


def _rms(x, g):
    ms = jnp.mean(x * x, axis=-1, keepdims=True)
    return x * lax.rsqrt(ms + NORM_EPS) * g


def _ffn_kernel(x_ref, g_ref, wg_ref, wu_ref, wo_ref, gn_ref, *rest, emit_hn):
    if emit_hn:
        o_ref, hn_ref, xn_s, acc_s = rest
    else:
        o_ref, xn_s, acc_s = rest
    j = pl.program_id(1)

    @pl.when(j == 0)
    def _():
        xn_s[...] = _rms(x_ref[...], g_ref[...]).astype(BF16)
        acc_s[...] = jnp.zeros_like(acc_s)

    xn = xn_s[...]
    gate = _dot(xn, wg_ref[...])
    up = _dot(xn, wu_ref[...])
    act = (gate * jax.nn.sigmoid(gate) * up).astype(BF16)
    acc_s[...] += _dot(act, wo_ref[...])

    @pl.when(j == pl.num_programs(1) - 1)
    def _():
        y = x_ref[...] + 0.5 * acc_s[...]
        o_ref[...] = y
        if emit_hn:
            hn_ref[...] = _rms(y, gn_ref[...]).astype(BF16)


def _ffn(x, g, w_in, w_out, g_next, emit_hn):
    rows = x.shape[0]
    tm = _tile(rows, 1024)
    nf = D_FF // FF_TILE
    out_shape = [jax.ShapeDtypeStruct((rows, D_MODEL), F32)]
    out_specs = [pl.BlockSpec((tm, D_MODEL), lambda i, j: (i, 0))]
    if emit_hn:
        out_shape.append(jax.ShapeDtypeStruct((rows, D_MODEL), BF16))
        out_specs.append(pl.BlockSpec((tm, D_MODEL), lambda i, j: (i, 0)))
    res = pl.pallas_call(
        functools.partial(_ffn_kernel, emit_hn=emit_hn),
        grid=(rows // tm, nf),
        in_specs=[
            pl.BlockSpec((tm, D_MODEL), lambda i, j: (i, 0)),
            pl.BlockSpec((1, D_MODEL), lambda i, j: (0, 0)),
            pl.BlockSpec((D_MODEL, FF_TILE), lambda i, j: (0, j)),
            pl.BlockSpec((D_MODEL, FF_TILE), lambda i, j: (0, j + nf)),
            pl.BlockSpec((FF_TILE, D_MODEL), lambda i, j: (j, 0)),
            pl.BlockSpec((1, D_MODEL), lambda i, j: (0, 0)),
        ],
        out_specs=out_specs,
        out_shape=out_shape,
        scratch_shapes=[pltpu.VMEM((tm, D_MODEL), BF16), pltpu.VMEM((tm, D_MODEL), F32)],
        compiler_params=_params(("parallel", "arbitrary")),
    )(x, g, w_in, w_in, w_out, g_next)
    return res if emit_hn else (res[0], None)


def _matmul_kernel(a_ref, b_ref, o_ref):
    o_ref[...] = _dot(a_ref[...], b_ref[...])


def _matmul(a, b, tn):
    rows, k = a.shape
    n = b.shape[1]
    tm = _tile(rows, 1024)
    return pl.pallas_call(
        _matmul_kernel,
        grid=(rows // tm, n // tn),
        in_specs=[pl.BlockSpec((tm, k), lambda i, j: (i, 0)), pl.BlockSpec((k, tn), lambda i, j: (0, j))],
        out_specs=pl.BlockSpec((tm, tn), lambda i, j: (i, j)),
        out_shape=jax.ShapeDtypeStruct((rows, n), F32),
        compiler_params=_params(("parallel", "arbitrary")),
    )(a, b)


def _qk_prep_kernel(q_ref, k_ref, f_ref, qg_ref, kg_ref, bf_ref, ones_ref, qn_ref, kn_ref, lf_ref, lfp_ref):
    ones_bd = ones_ref[...]

    def headnorm(x, g):
        ms = _segsum(x * x, ones_bd) * (1.0 / HEAD_DIM)
        return x * lax.rsqrt(ms + NORM_EPS) * g

    qn_ref[...] = (headnorm(q_ref[...], qg_ref[...]) * (HEAD_DIM ** -0.5)).astype(BF16)
    kn_ref[...] = headnorm(k_ref[...], kg_ref[...])
    lf = -_softplus(-(f_ref[...] + bf_ref[...]))
    lf_ref[...] = lf
    lane = lax.broadcasted_iota(jnp.int32, lf.shape, 1)
    for hp in range(N_HEADS // 2):
        a = lf[:, 2 * hp:2 * hp + 1]
        b = lf[:, 2 * hp + 1:2 * hp + 2]
        lfp_ref[:, hp * LANE:(hp + 1) * LANE] = jnp.where(lane == 0, a, jnp.where(lane == 1, b, 0.0))


def _qk_prep(qkvu, qg, kg, bf, ones_bd):
    rows = qkvu.shape[0]
    tm = _tile(rows, 1024)
    small = lambda w: pl.BlockSpec((1, w), lambda i: (0, 0))
    return pl.pallas_call(
        _qk_prep_kernel,
        grid=(rows // tm,),
        in_specs=[
            pl.BlockSpec((tm, MIX), lambda i: (i, 0)),
            pl.BlockSpec((tm, MIX), lambda i: (i, 1)),
            pl.BlockSpec((tm, LANE), lambda i: (i, 4 * MIX // LANE)),
            small(MIX), small(MIX), small(LANE),
            pl.BlockSpec((MIX, MIX), lambda i: (0, 0)),
        ],
        out_specs=[
            pl.BlockSpec((tm, MIX), lambda i: (i, 0)),
            pl.BlockSpec((tm, MIX), lambda i: (i, 0)),
            pl.BlockSpec((tm, LANE), lambda i: (i, 0)),
            pl.BlockSpec((tm, MIX), lambda i: (i, 0)),
        ],
        out_shape=[
            jax.ShapeDtypeStruct((rows, MIX), BF16),
            jax.ShapeDtypeStruct((rows, MIX), F32),
            jax.ShapeDtypeStruct((rows, LANE), F32),
            jax.ShapeDtypeStruct((rows, MIX), F32),
        ],
        compiler_params=_params(("parallel",)),
    )(qkvu, qkvu, qkvu, qg, kg, bf, ones_bd)


def _fox_attn_kernel(q_ref, k_ref, v_ref, lf_ref, o_ref, c_s):
    tp = q_ref.shape[0]
    cb = LANE
    row = lax.broadcasted_iota(jnp.int32, (cb, cb), 0)
    col = lax.broadcasted_iota(jnp.int32, (cb, cb), 1)
    tri = (row >= col).astype(F32)
    carry = jnp.zeros((1, LANE), F32)
    for blk in range(tp // cb):
        cblk = _dot_exact_lhs(tri, lf_ref[blk * cb:(blk + 1) * cb, :]) + carry
        c_s[blk * cb:(blk + 1) * cb, :] = cblk
        carry = cblk[cb - 1:cb, :]
    c = c_s[...]
    c_t = c.T
    q = q_ref[...]
    k = k_ref[...].astype(BF16)
    v = v_ref[...].astype(BF16)
    lane = lax.broadcasted_iota(jnp.int32, (1, LANE), 1)
    for r0 in range(0, tp, ATT_Q_TILE):
        r1 = min(r0 + ATT_Q_TILE, tp)
        tq = r1 - r0
        qi = lax.broadcasted_iota(jnp.int32, (tq, r1), 0) + r0
        ki = lax.broadcasted_iota(jnp.int32, (tq, r1), 1)
        causal = ki <= qi
        outs = []
        for e in range(2):
            head = (lane // HEAD_DIM) == e
            qe = jnp.where(head, q[r0:r1, :], jnp.zeros((), BF16))
            s = _dot(qe, k[:r1, :], NT)
            s = s + c[r0:r1, e:e + 1] - c_t[e:e + 1, :r1]
            s = jnp.where(causal, s, -1e30)
            m = jnp.max(s, axis=-1, keepdims=True)
            p = jnp.exp(s - m)
            l = jnp.sum(p, axis=-1, keepdims=True)
            outs.append(_dot(p.astype(BF16), v[:r1, :]) / l)
        o_ref[r0:r1, :] = jnp.where((lane // HEAD_DIM) == 0, outs[0], outs[1])


def _fox_attention(qn, kn, qkvu, lfp, bsz, tp):
    hp = N_HEADS // 2
    out = pl.pallas_call(
        _fox_attn_kernel,
        grid=(bsz, hp),
        in_specs=[
            pl.BlockSpec((None, tp, LANE), lambda b, h: (b, 0, h)),
            pl.BlockSpec((None, tp, LANE), lambda b, h: (b, 0, h)),
            pl.BlockSpec((None, tp, LANE), lambda b, h: (b, 0, 2 * MIX // LANE + h)),
            pl.BlockSpec((None, tp, LANE), lambda b, h: (b, 0, h)),
        ],
        out_specs=pl.BlockSpec((None, tp, LANE), lambda b, h: (b, 0, h)),
        out_shape=jax.ShapeDtypeStruct((bsz, tp, MIX), F32),
        scratch_shapes=[pltpu.VMEM((tp, LANE), F32)],
        compiler_params=_params(("parallel", "parallel")),
    )(qn.reshape(bsz, tp, MIX), kn.reshape(bsz, tp, MIX), qkvu.reshape(bsz, tp, QKVU_COLS),
      lfp.reshape(bsz, tp, MIX))
    return out.reshape(bsz * tp, MIX)


def _fox_decode_kernel(pt_ref, q_ref, kn_ref, vn_ref, lfn_ref, *rest, pages_per_step):
    del pt_ref
    g = pages_per_step
    k_refs, v_refs, lf_refs = rest[:g], rest[g:2 * g], rest[2 * g:3 * g]
    o_ref, m_s, l_s, acc_s, carry_s = rest[3 * g:]
    p = pl.program_id(1)
    hrow = lax.broadcasted_iota(jnp.int32, (N_HEADS, MIX), 0)
    hlane = lax.broadcasted_iota(jnp.int32, (N_HEADS, MIX), 1) // HEAD_DIM
    headmask = hrow == hlane
    qb = jnp.where(headmask, q_ref[...], 0.0)

    @pl.when(p == 0)
    def _():
        m_s[...] = jnp.sum(qb * kn_ref[...], axis=1, keepdims=True)
        l_s[...] = jnp.ones_like(l_s)
        acc_s[...] = jnp.where(headmask, vn_ref[...], 0.0)
        carry_s[...] = lfn_ref[...]

    srow = lax.broadcasted_iota(jnp.int32, (PAGE, PAGE), 0)
    scol = lax.broadcasted_iota(jnp.int32, (PAGE, PAGE), 1)
    later = (srow > scol).astype(F32)
    qbb = qb.astype(BF16)
    carry = carry_s[...]
    logits = []
    for i in range(g):
        lf = lf_refs[i][...]
        kt = k_refs[i][...].reshape(MIX, PAGE).astype(BF16)
        logits.append(_dot(qbb, kt) + (carry + _dot_exact_rhs(lf, later)))
        carry = carry + jnp.sum(lf, axis=1, keepdims=True)
    carry_s[...] = carry
    m_old = m_s[...]
    m_new = m_old
    for s in logits:
        m_new = jnp.maximum(m_new, jnp.max(s, axis=1, keepdims=True))
    alpha = jnp.exp(m_old - m_new)
    l = l_s[...] * alpha
    acc = acc_s[...] * alpha
    for i in range(g):
        pr = jnp.exp(logits[i] - m_new)
        l = l + jnp.sum(pr, axis=1, keepdims=True)
        acc = acc + _dot(pr.astype(BF16), v_refs[i][...].reshape(MIX, PAGE).astype(BF16), NT)
    m_s[...], l_s[...], acc_s[...] = m_new, l, acc

    @pl.when(p == pl.num_programs(1) - 1)
    def _():
        o_ref[...] = jnp.sum(jnp.where(headmask, acc / l, 0.0), axis=0, keepdims=True)


def _fox_decode(layer, q, kn, vn, lfn, cache_kt, cache_vt, cache_lft, page_table):
    bsz, n_pages = page_table.shape
    g = DECODE_PAGES_PER_STEP
    assert n_pages % g == 0

    def page_spec(block, i):
        return pl.BlockSpec((None, None) + block,
                            lambda b, p, pt: (layer, pt[b, n_pages - 1 - (p * g + i)]) + (0,) * len(block))

    tok = lambda r, w: pl.BlockSpec((None, r, w), lambda b, p, pt: (b, 0, 0))
    in_specs = [tok(1, MIX), tok(1, MIX), tok(1, MIX), tok(N_HEADS, 1)]
    in_specs += [page_spec((N_HEADS, HEAD_DIM, PAGE), i) for i in range(g)]
    in_specs += [page_spec((N_HEADS, HEAD_DIM, PAGE), i) for i in range(g)]
    in_specs += [page_spec((N_HEADS, PAGE), i) for i in range(g)]
    grid_spec = pltpu.PrefetchScalarGridSpec(
        num_scalar_prefetch=1,
        grid=(bsz, n_pages // g),
        in_specs=in_specs,
        out_specs=tok(1, MIX),
        scratch_shapes=[pltpu.VMEM((N_HEADS, 1), F32), pltpu.VMEM((N_HEADS, 1), F32),
                        pltpu.VMEM((N_HEADS, MIX), F32), pltpu.VMEM((N_HEADS, 1), F32)],
    )
    return pl.pallas_call(
        functools.partial(_fox_decode_kernel, pages_per_step=g),
        grid_spec=grid_spec,
        out_shape=jax.ShapeDtypeStruct((bsz, 1, MIX), F32),
        compiler_params=_params(("parallel", "arbitrary")),
    )(page_table, q, kn, vn, lfn, *([cache_kt] * g), *([cache_vt] * g), *([cache_lft] * g))


def _rwkv_prep_kernel(rw_ref, prev_ref, sh0_ref, mu_ref, w0_ref, w2_ref, a0_ref, a2_ref, g2_ref,
                      kk_ref, ka_ref, rk_ref, ones_ref,
                      r_out, lw_out, k_out, v_out, kk_out, b_out, g_out, bonus_out):
    i = pl.program_id(1)
    rw = rw_ref[...]
    tm = rw.shape[0]
    first = jnp.where(i == 0, sh0_ref[...], prev_ref[SUBLANE - 1:SUBLANE, :])
    trow = lax.broadcasted_iota(jnp.int32, (tm, 1), 0)
    prev = jnp.where(trow == 0, first, pltpu.roll(rw, 1, 0))
    z = rw + (prev - rw) * mu_ref[...]
    r = z[:, 0:MIX]
    k = z[:, MIX:2 * MIX]
    v = z[:, 2 * MIX:3 * MIX]
    xwa = z[:, 3 * MIX:3 * MIX + 2 * LORA_WA]
    xg = z[:, 3 * MIX + 2 * LORA_WA:]
    ones_bd = ones_ref[...]
    w = -_softplus(-(w0_ref[...] + _dot(jnp.tanh(xwa).astype(BF16), w2_ref[...]))) - 0.5
    lw_out[...] = -jnp.exp(w)
    a = jax.nn.sigmoid(a0_ref[...] + _dot(xwa.astype(BF16), a2_ref[...]))
    g_out[...] = _dot(jax.nn.sigmoid(xg).astype(BF16), g2_ref[...])
    kk = k * kk_ref[...]
    kk = kk / jnp.maximum(jnp.sqrt(_segsum(kk * kk, ones_bd)), 1e-12)
    k2 = k * (1.0 + (a - 1.0) * ka_ref[...])
    r_out[...] = r
    k_out[...] = k2
    v_out[...] = v
    kk_out[...] = kk
    b_out[...] = kk * a
    bonus_out[...] = _segsum(r * k2 * rk_ref[...], ones_bd) * v


def _rwkv_prep(rwp, shift0, prm, bsz, tp):
    tm = _tile(tp, 544)
    nb = tm // SUBLANE
    small = lambda a: pl.BlockSpec(a.shape, lambda b, i: (0, 0))
    consts = [prm["mu"], prm["w0"], prm["w2"], prm["a0"], prm["a2"], prm["g2"], prm["k_k"], prm["k_a"],
              prm["r_k"], prm["ones_bd"]]
    rw3 = rwp.reshape(bsz, tp, RW_PAD)
    outs = pl.pallas_call(
        _rwkv_prep_kernel,
        grid=(bsz, tp // tm),
        in_specs=[
            pl.BlockSpec((None, tm, RW_PAD), lambda b, i: (b, i, 0)),
            pl.BlockSpec((None, SUBLANE, RW_PAD), lambda b, i: (b, jnp.maximum(i * nb - 1, 0), 0)),
            pl.BlockSpec((None, 1, RW_PAD), lambda b, i: (b, 0, 0)),
        ] + [small(c) for c in consts],
        out_specs=[pl.BlockSpec((None, tm, MIX), lambda b, i: (b, i, 0))] * 8,
        out_shape=[jax.ShapeDtypeStruct((bsz, tp, MIX), F32)] * 8,
        compiler_params=_params(("parallel", "parallel")),
    )(rw3, rw3, shift0, *consts)
    return outs


def _rwkv_scan_kernel(r_ref, lw_ref, k_ref, v_ref, kk_ref, b_ref, s0_ref, o_ref, s_out, s_s, *, chunk, t_valid):
    c = pl.program_id(1)
    n = chunk

    @pl.when(c == 0)
    def _():
        s_s[...] = s0_ref[...]

    trow = lax.broadcasted_iota(jnp.int32, (n, 1), 0) + c * n
    valid = trow < t_valid
    lw = jnp.where(valid, lw_ref[...], 0.0)
    kk = jnp.where(valid, kk_ref[...], 0.0)
    bb = jnp.where(valid, b_ref[...], 0.0)
    kx = jnp.where(valid, k_ref[...], 0.0)
    vx = jnp.where(valid, v_ref[...], 0.0)
    rx = r_ref[...]

    row = lax.broadcasted_iota(jnp.int32, (n, n), 0)
    col = lax.broadcasted_iota(jnp.int32, (n, n), 1)
    incl = row >= col
    strict = row > col
    eye = (row == col).astype(F32)
    ci = _dot_exact_lhs(incl.astype(F32), lw)
    ce = ci - lw
    kkd = kk * jnp.exp(ce)
    rd = rx * jnp.exp(ci)
    einv = jnp.exp(-ci)
    bd = bb * einv
    kd = kx * einv
    cl = ci[n - 1:n, :]
    wl = jnp.exp(cl)
    dl = jnp.exp(cl - ci)
    bdw = bb * dl
    kdw = kx * dl
    row2 = lax.broadcasted_iota(jnp.int32, (n, 2 * n), 0)
    col2 = lax.broadcasted_iota(jnp.int32, (n, 2 * n), 1)
    col2 = jnp.where(col2 >= n, col2 - n, col2)
    strict2 = row2 > col2
    incl2 = row2 >= col2

    hs = range(N_HEADS)
    sls = [slice(h * HEAD_DIM, (h + 1) * HEAD_DIM) for h in hs]
    s_all = [s_s[h] for h in hs]
    a_m = [jnp.concatenate([kkd[:, sl], rd[:, sl]], axis=0) for sl in sls]
    b_m = [jnp.concatenate([bd[:, sl], kd[:, sl]], axis=0) for sl in sls]
    pm = [_dot1(a_m[h], b_m[h], NT) for h in hs]
    p_top = [jnp.where(strict2, pm[h][:n, :], 0.0) for h in hs]
    p_bot = [jnp.where(incl2, pm[h][n:, :], 0.0) for h in hs]
    gm = [_dot3(a_m[h], s_all[h], NT) for h in hs]
    x = [-p_top[h][:, :n] for h in hs]
    tinv = [eye + x[h] for h in hs]
    span = 2
    while span < n:
        x = [_dot1(x[h], x[h]) for h in hs]
        tinv = [tinv[h] + _dot1(tinv[h], x[h]) for h in hs]
        span *= 2
    v_h = [vx[:, sl] for sl in sls]
    rhs = [gm[h][:n, :] + _dot1(p_top[h][:, n:], v_h[h]) for h in hs]
    u = [-_dot1(tinv[h], rhs[h]) for h in hs]
    uv = [jnp.concatenate([u[h], v_h[h]], axis=0) for h in hs]
    outs = [gm[h][n:, :] + _dot1(p_bot[h], uv[h]) for h in hs]
    bk = [jnp.concatenate([bdw[:, sl], kdw[:, sl]], axis=0) for sl in sls]
    s_new = [s_all[h] * wl[:, sls[h]] + _dot3(uv[h], bk[h], TN) for h in hs]
    for h in hs:
        s_s[h] = s_new[h]
    o_ref[...] = jnp.concatenate(outs, axis=1)

    @pl.when(c == pl.num_programs(1) - 1)
    def _():
        s_out[...] = s_s[...]


def _rwkv_scan(r, lw, k, v, kk, b, s0, bsz, tp, t_valid, chunk):
    blk = pl.BlockSpec((None, chunk, MIX), lambda bi, c: (bi, c, 0))
    st = pl.BlockSpec((None, N_HEADS, HEAD_DIM, HEAD_DIM), lambda bi, c: (bi, 0, 0, 0))
    o, s_out = pl.pallas_call(
        functools.partial(_rwkv_scan_kernel, chunk=chunk, t_valid=t_valid),
        grid=(bsz, tp // chunk),
        in_specs=[blk] * 6 + [st],
        out_specs=[blk, st],
        out_shape=[jax.ShapeDtypeStruct((bsz, tp, MIX), F32),
                   jax.ShapeDtypeStruct((bsz, N_HEADS, HEAD_DIM, HEAD_DIM), F32)],
        scratch_shapes=[pltpu.VMEM((N_HEADS, HEAD_DIM, HEAD_DIM), F32)],
        compiler_params=_params(("parallel", "arbitrary")),
    )(r, lw, k, v, kk, b, s0)
    return o.reshape(bsz * tp, MIX), s_out


def _s5_kernel(u_ref, x0_ref, a_ref, bm_ref, cm_ref, d_ref, gw_ref, gb_ref, o_ref, xo_ref, buf_s, st_s,
               *, t_valid):
    c = pl.program_id(0)
    bsz, tc, _ = u_ref.shape
    tcp = buf_s.shape[1] // bsz
    half = S5_LANES // 2
    hc = MIX // 2
    nblk = S5_LANES // LANE
    hblk = half // LANE

    @pl.when(c == 0)
    def _():
        st_s[...] = x0_ref[...]

    u2 = u_ref[...].reshape(bsz * tc, MIX)
    ub = u2.astype(BF16)
    for part in range(2):
        for j in range(2):
            col0 = part * S5_LANES + j * half
            res = _dot(ub[:, j * hc:(j + 1) * hc], bm_ref[j * hc:(j + 1) * hc, col0:col0 + half])
            for k in range(hblk):
                for b in range(bsz):
                    buf_s[col0 // LANE + k, b * tcp:b * tcp + tc, :] = res[b * tc:(b + 1) * tc,
                                                                           k * LANE:(k + 1) * LANE]

    gb = max(1, 2 * nblk // bsz)
    for g0 in range(0, nblk, gb):
        blks = list(range(g0, g0 + gb))
        lane = lambda k: slice(k * LANE, (k + 1) * LANE)
        ar = [jnp.broadcast_to(a_ref[:, lane(k)], (bsz, LANE)) for k in blks]
        ai = [jnp.broadcast_to(a_ref[:, lane(nblk + k)], (bsz, LANE)) for k in blks]

        def body(t, carry, blks=blks, ar=ar, ai=ai):
            rows = pl.ds(t, bsz, stride=tcp)
            out = []
            for n, k in enumerate(blks):
                xr, xi = carry[n]
                nxr = ar[n] * xr - ai[n] * xi + buf_s[k, rows, :]
                nxi = ar[n] * xi + ai[n] * xr + buf_s[nblk + k, rows, :]
                buf_s[k, rows, :] = nxr
                buf_s[nblk + k, rows, :] = nxi
                out.append((nxr, nxi))
            return tuple(out)

        init = tuple((st_s[:, lane(k)], st_s[:, lane(nblk + k)]) for k in blks)
        fin = lax.fori_loop(0, tc, body, init)
        for n, k in enumerate(blks):
            st_s[:, lane(k)] = fin[n][0]
            st_s[:, lane(nblk + k)] = fin[n][1]

    c_valid, t_in = divmod(t_valid - 1, tc)

    @pl.when(c == c_valid)
    def _():
        for k in range(2 * nblk):
            xo_ref[:, k * LANE:(k + 1) * LANE] = buf_s[k, pl.ds(t_in, bsz, stride=tcp), :]

    ys = []
    for j in range(2):
        acc = None
        for part in range(2):
            col0 = part * S5_LANES + j * half
            xs = jnp.concatenate(
                [jnp.concatenate([buf_s[col0 // LANE + k, b * tcp:b * tcp + tc, :] for b in range(bsz)], axis=0)
                 for k in range(hblk)], axis=1)
            d = _dot(xs.astype(BF16), cm_ref[col0:col0 + half, j * hc:(j + 1) * hc])
            acc = d if acc is None else acc + d
        ys.append(acc)
    y = jnp.concatenate(ys, axis=1) + d_ref[...] * u2
    y = 0.5 * y * (1.0 + jnp.tanh(0.7978845608028654 * (y + 0.044715 * (y * y * y))))
    gate = jax.nn.sigmoid(_dot(y.astype(BF16), gw_ref[...]) + gb_ref[...])
    o_ref[...] = (y * gate).reshape(bsz, tc, MIX)


def _s5(qkvu, x0, prm, bsz, tp, t_valid, tc):
    n2 = 2 * S5_LANES
    full = lambda a: pl.BlockSpec(a.shape, lambda c: (0,) * a.ndim)
    consts = [prm["s5_a"], prm["s5_bm"], prm["s5_cm"], prm["s5_d"], prm["s5_gw"], prm["s5_gb"]]
    o, xo = pl.pallas_call(
        functools.partial(_s5_kernel, t_valid=t_valid),
        grid=(tp // tc,),
        in_specs=[pl.BlockSpec((bsz, tc, MIX), lambda c: (0, c, 3)), full(x0)] + [full(a) for a in consts],
        out_specs=[pl.BlockSpec((bsz, tc, MIX), lambda c: (0, c, 0)), pl.BlockSpec((bsz, n2), lambda c: (0, 0))],
        out_shape=[jax.ShapeDtypeStruct((bsz, tp, MIX), F32), jax.ShapeDtypeStruct((bsz, n2), F32)],
        scratch_shapes=[pltpu.VMEM((n2 // LANE, bsz * (tc + S5_SLAB_PAD), LANE), F32),
                        pltpu.VMEM((bsz, n2), F32)],
        compiler_params=_params(("arbitrary",)),
    )(qkvu.reshape(bsz, tp, QKVU_COLS), x0, *consts)
    return o.reshape(bsz * tp, MIX), xo


def _merge_kernel(x_ref, oa_ref, ob_ref, g_ref, bonus_ref, oc_ref, gates_ref, lw_ref, lb_ref, ones_ref,
                  wb_ref, wo_ref, o_ref):
    ones_bd = ones_ref[...]
    o = ob_ref[...]
    mu = _segsum(o, ones_bd) * (1.0 / HEAD_DIM)
    dlt = o - mu
    var = _segsum(dlt * dlt, ones_bd) * (1.0 / HEAD_DIM)
    ob = (dlt * lax.rsqrt(var + LNX_EPS) * lw_ref[...] + lb_ref[...] + bonus_ref[...]) * g_ref[...]
    merged = jax.nn.sigmoid(gates_ref[:, 0:D_MODEL]) * _dot(oa_ref[...].astype(BF16), wb_ref[0:MIX, :])
    merged += jax.nn.sigmoid(gates_ref[:, D_MODEL:2 * D_MODEL]) * _dot(ob.astype(BF16), wb_ref[MIX:2 * MIX, :])
    merged += jax.nn.sigmoid(gates_ref[:, 2 * D_MODEL:]) * _dot(oc_ref[...].astype(BF16), wb_ref[2 * MIX:, :])
    o_ref[...] = x_ref[...] + _dot(merged.astype(BF16), wo_ref[...])


def _merge(x, oa, ob, g, bonus, oc, gates, prm):
    rows = x.shape[0]
    tm = _tile(rows, 512)
    row = lambda w: pl.BlockSpec((tm, w), lambda i: (i, 0))
    full = lambda a: pl.BlockSpec(a.shape, lambda i: (0, 0))
    consts = [prm["lnx_w"], prm["lnx_b"], prm["ones_bd"], prm["w_branch"], prm["w_out"]]
    return pl.pallas_call(
        _merge_kernel,
        grid=(rows // tm,),
        in_specs=[row(D_MODEL), row(MIX), row(MIX), row(MIX), row(MIX), row(MIX), row(3 * D_MODEL)]
        + [full(c) for c in consts],
        out_specs=row(D_MODEL),
        out_shape=jax.ShapeDtypeStruct((rows, D_MODEL), F32),
        compiler_params=_params(("parallel",)),
    )(x, oa, ob, g, bonus, oc, gates, *consts)


def _layer_params(l, p):
    bf = lambda a: a.astype(BF16)
    row = lambda a, w=None: (a if w is None else jnp.pad(a, (0, w - a.shape[0]))).reshape(1, -1).astype(F32)
    w_in = p["w_in"][l]
    o_f = 3 * MIX
    o_rw = o_f + N_HEADS
    o_u = o_rw + RW_COLS
    o_g = o_u + MIX
    w_qkvu = jnp.concatenate(
        [w_in[:, :o_f], w_in[:, o_u:o_g], jnp.pad(w_in[:, o_f:o_rw], ((0, 0), (0, LANE - N_HEADS)))], axis=1)
    w_rw = jnp.pad(w_in[:, o_rw:o_u], ((0, 0), (0, RW_PAD - RW_COLS)))
    w_gates = w_in[:, o_g:]

    ar = p["s5_a_re"][l].astype(F32)
    ai = p["s5_a_im"][l].astype(F32)
    dt = jnp.exp(p["s5_log_dt"][l].astype(F32))[:, None]
    mag = jnp.exp(dt * ar)
    abar_re = mag * jnp.cos(dt * ai)
    abar_im = mag * jnp.sin(dt * ai)
    den = ar * ar + ai * ai
    nr, ni = abar_re - 1.0, abar_im
    z_re = (nr * ar + ni * ai) / den
    z_im = (ni * ar - nr * ai) / den
    b_re = p["s5_b_re"][l].astype(F32)
    b_im = p["s5_b_im"][l].astype(F32)
    bb_re = z_re[..., None] * b_re - z_im[..., None] * b_im
    bb_im = z_re[..., None] * b_im + z_im[..., None] * b_re
    eye = jnp.eye(S5_GROUPS, dtype=F32)
    blk_in = lambda m: jnp.einsum("gpc,gh->gchp", m, eye).reshape(MIX, S5_LANES)
    blk_out = lambda m: jnp.einsum("gcp,gh->gphc", m, eye).reshape(S5_LANES, MIX)
    s5_bm = jnp.concatenate([blk_in(bb_re), blk_in(bb_im)], axis=1)
    s5_cm = jnp.concatenate([blk_out(p["s5_c_re"][l].astype(F32)), -blk_out(p["s5_c_im"][l].astype(F32))], axis=0)

    hd = lax.broadcasted_iota(jnp.int32, (MIX, MIX), 0) // HEAD_DIM
    hd_t = lax.broadcasted_iota(jnp.int32, (MIX, MIX), 1) // HEAD_DIM
    return dict(
        ffn1_norm=row(p["ffn1_norm"][l]), ffn1_w_in=bf(p["ffn1_w_in"][l]), ffn1_w_out=bf(p["ffn1_w_out"][l]),
        ffn2_norm=row(p["ffn2_norm"][l]), ffn2_w_in=bf(p["ffn2_w_in"][l]), ffn2_w_out=bf(p["ffn2_w_out"][l]),
        mix_norm=row(p["mix_norm"][l]),
        w_qkvu=bf(w_qkvu), w_rw=bf(w_rw), w_gates=bf(w_gates),
        q_gain=row(jnp.tile(p["fox_q_gain"][l], N_HEADS)), k_gain=row(jnp.tile(p["fox_k_gain"][l], N_HEADS)),
        b_f=row(p["fox_b_f"][l], LANE),
        ones_bd=(hd == hd_t).astype(BF16),
        mu=row(p["rwkv_mu"][l], RW_PAD), w0=row(p["rwkv_w0"][l]), a0=row(p["rwkv_a0"][l]),
        w2=bf(jnp.pad(p["rwkv_w2"][l], ((0, LORA_WA), (0, 0)))),
        a2=bf(jnp.pad(p["rwkv_a2"][l], ((LORA_WA, 0), (0, 0)))),
        g2=bf(jnp.pad(p["rwkv_g2"][l], ((0, RW_PAD - RW_COLS), (0, 0)))),
        k_k=row(p["rwkv_k_k"][l]), k_a=row(p["rwkv_k_a"][l]), r_k=row(p["rwkv_r_k"][l].reshape(-1)),
        lnx_w=row(p["rwkv_lnx_w"][l]), lnx_b=row(p["rwkv_lnx_b"][l]),
        s5_a=jnp.concatenate([abar_re.reshape(1, -1), abar_im.reshape(1, -1)], axis=1),
        s5_bm=bf(s5_bm), s5_cm=bf(s5_cm), s5_d=row(p["s5_d"][l].reshape(-1)),
        s5_gw=bf(p["s5_glu_w"][l]), s5_gb=row(p["s5_glu_b"][l]),
        w_branch=bf(p["w_branch"][l]), w_out=bf(p["w_out"][l]),
    )


def _mixer_inputs(x, prm):
    x, hn = _ffn(x, prm["ffn1_norm"], prm["ffn1_w_in"], prm["ffn1_w_out"], prm["mix_norm"], True)
    qkvu = _matmul(hn, prm["w_qkvu"], QKVU_COLS)
    rwp = _matmul(hn, prm["w_rw"], RW_PAD)
    gates = _matmul(hn, prm["w_gates"], D_MODEL)
    qn, kn, lf, lfp = _qk_prep(qkvu, prm["q_gain"], prm["k_gain"], prm["b_f"], prm["ones_bd"])
    return x, qkvu, rwp, gates, qn, kn, lf, lfp


def _layer_tail(x, oa, qkvu, rwp, gates, prm, shift0, s0, x0, bsz, tp, t_valid, chunk, tc):
    r, lw, k2, v, kk, b, g, bonus = _rwkv_prep(rwp, shift0, prm, bsz, tp)
    ob, s_out = _rwkv_scan(r, lw, k2, v, kk, b, s0, bsz, tp, t_valid, chunk)
    oc, xo = _s5(qkvu, x0, prm, bsz, tp, t_valid, tc)
    flat = lambda a: a.reshape(bsz * tp, MIX)
    x = _merge(x, oa, ob, flat(g), flat(bonus), oc, gates, prm)
    x, _ = _ffn(x, prm["ffn2_norm"], prm["ffn2_w_in"], prm["ffn2_w_out"], prm["ffn2_norm"], False)
    return x, s_out, xo


def kernel(x_prompt, x_sample, cache_k, cache_v, cache_logf, page_table, state_rwkv, state_shift, state_s5_re, state_s5_im, meta_tokens, ffn1_norm, ffn1_w_in, ffn1_w_out, mix_norm, w_in, fox_b_f, fox_q_gain, fox_k_gain, rwkv_mu, rwkv_w0, rwkv_w2, rwkv_a0, rwkv_a2, rwkv_g2, rwkv_k_k, rwkv_k_a, rwkv_r_k, rwkv_lnx_w, rwkv_lnx_b, s5_a_re, s5_a_im, s5_log_dt, s5_b_re, s5_b_im, s5_c_re, s5_c_im, s5_d, s5_glu_w, s5_glu_b, w_branch, w_out, ffn2_norm, ffn2_w_in, ffn2_w_out):
    p = dict(ffn1_norm=ffn1_norm, ffn1_w_in=ffn1_w_in, ffn1_w_out=ffn1_w_out, mix_norm=mix_norm,
             w_in=w_in, fox_b_f=fox_b_f, fox_q_gain=fox_q_gain, fox_k_gain=fox_k_gain,
             rwkv_mu=rwkv_mu, rwkv_w0=rwkv_w0, rwkv_w2=rwkv_w2, rwkv_a0=rwkv_a0, rwkv_a2=rwkv_a2,
             rwkv_g2=rwkv_g2, rwkv_k_k=rwkv_k_k, rwkv_k_a=rwkv_k_a, rwkv_r_k=rwkv_r_k,
             rwkv_lnx_w=rwkv_lnx_w, rwkv_lnx_b=rwkv_lnx_b, s5_a_re=s5_a_re, s5_a_im=s5_a_im,
             s5_log_dt=s5_log_dt, s5_b_re=s5_b_re, s5_b_im=s5_b_im, s5_c_re=s5_c_re, s5_c_im=s5_c_im,
             s5_d=s5_d, s5_glu_w=s5_glu_w, s5_glu_b=s5_glu_b, w_branch=w_branch, w_out=w_out,
             ffn2_norm=ffn2_norm, ffn2_w_in=ffn2_w_in, ffn2_w_out=ffn2_w_out)
    depth = w_in.shape[0]
    prms = [_layer_params(l, p) for l in range(depth)]

    bp, seq, _ = x_prompt.shape
    t_p = seq + N_META
    tp = -(-t_p // SEQ_ALIGN) * SEQ_ALIGN
    meta = jnp.broadcast_to(meta_tokens.astype(F32)[None], (bp, N_META, D_MODEL))
    xp = jnp.concatenate([meta, x_prompt], axis=1)
    xp = jnp.pad(xp, ((0, 0), (0, tp - t_p), (0, 0))).reshape(bp * tp, D_MODEL)
    z_shift = jnp.zeros((bp, 1, RW_PAD), F32)
    z_s = jnp.zeros((bp, N_HEADS, HEAD_DIM, HEAD_DIM), F32)
    z_x = jnp.zeros((bp, 2 * S5_LANES), F32)
    seqs = lambda a, w: a.reshape(bp, tp, w)
    pk, pv, plf, prw, psh, pre, pim = [], [], [], [], [], [], []
    for l in range(depth):
        prm = prms[l]
        xp, qkvu, rwp, gates, qn, kn, lf, lfp = _mixer_inputs(xp, prm)
        oa = _fox_attention(qn, kn, qkvu, lfp, bp, tp)
        xp, s_out, xo = _layer_tail(xp, oa, qkvu, rwp, gates, prm, z_shift, z_s, z_x, bp, tp, t_p,
                                    RWKV_CHUNK, S5_CHUNK)
        pk.append(seqs(kn, MIX)[:, :t_p].reshape(bp, t_p, N_HEADS, HEAD_DIM))
        pv.append(seqs(qkvu, QKVU_COLS)[:, :t_p, 2 * MIX:3 * MIX].reshape(bp, t_p, N_HEADS, HEAD_DIM))
        plf.append(seqs(lf, LANE)[:, :t_p, :N_HEADS])
        prw.append(s_out)
        psh.append(seqs(rwp, RW_PAD)[:, t_p - 1, :RW_COLS])
        pre.append(xo[:, :S5_LANES].reshape(bp, S5_GROUPS, S5_STATE))
        pim.append(xo[:, S5_LANES:].reshape(bp, S5_GROUPS, S5_STATE))
    y_prompt = seqs(xp, D_MODEL)[:, N_META:t_p]

    bs = x_sample.shape[0]
    ts = SUBLANE
    xs = jnp.pad(x_sample, ((0, 0), (0, ts - 1), (0, 0))).reshape(bs * ts, D_MODEL)
    ck =cache_k.transpose(0, 1, 3, 4, 2)
    cv = cache_v.transpose(0, 1, 3, 4, 2)
    clf = cache_logf.transpose(0, 1, 3, 2)
    tok0 = lambda a, w: a.reshape(bs, ts, w)[:, 0]
    sk, sv, slf, srw, ssh, sre, sim = [], [], [], [], [], [], []
    for l in range(depth):
        prm = prms[l]
        xs, qkvu, rwp, gates, qn, kn, lf, lfp = _mixer_inputs(xs, prm)
        q_new = tok0(qn, MIX).astype(F32)
        k_new = tok0(kn, MIX)
        v_new = tok0(qkvu, QKVU_COLS)[:, 2 * MIX:3 * MIX]
        lf_new = tok0(lf, LANE)[:, :N_HEADS]
        oa = _fox_decode(l, q_new[:, None], k_new[:, None], v_new[:, None], lf_new[:, :, None], ck, cv, clf,
                         page_table)
        oa = jnp.pad(oa, ((0, 0), (0, ts - 1), (0, 0))).reshape(bs * ts, MIX)
        shift0 = jnp.pad(state_shift[l].astype(F32), ((0, 0), (0, RW_PAD - RW_COLS)))[:, None]
        x0 = jnp.concatenate([state_s5_re[l].reshape(bs, S5_LANES), state_s5_im[l].reshape(bs, S5_LANES)],
                             axis=1).astype(F32)
        xs, s_out, xo = _layer_tail(xs, oa, qkvu, rwp, gates, prm, shift0, state_rwkv[l].astype(F32), x0,
                                    bs, ts, 1, ts, ts)
        sk.append(k_new.reshape(bs, 1, N_HEADS, HEAD_DIM))
        sv.append(v_new.reshape(bs, 1, N_HEADS, HEAD_DIM))
        slf.append(lf_new.reshape(bs, 1, N_HEADS))
        srw.append(s_out)
        ssh.append(tok0(rwp, RW_PAD)[:, :RW_COLS])
        sre.append(xo[:, :S5_LANES].reshape(bs, S5_GROUPS, S5_STATE))
        sim.append(xo[:, S5_LANES:].reshape(bs, S5_GROUPS, S5_STATE))
    y_sample = tok0(xs, D_MODEL).reshape(bs, 1, D_MODEL)

    st = lambda xs_: jnp.stack(xs_, axis=0)
    return (y_prompt, y_sample, st(pk), st(pv), st(plf), st(prw), st(psh), st(pre), st(pim),
            st(sk), st(sv), st(slf), st(srw), st(ssh), st(sre), st(sim))
```

```python
import functools

import jax
import jax.numpy as jnp
from jax import lax
from jax.experimental import pallas as pl
from jax.experimental.pallas import tpu as pltpu

F32 = jnp.float32
BF16 = jnp.bfloat16

D_MODEL = 1024
N_META = 16
HEAD_DIM = 64
MIX = D_MODEL // 2
N_HEADS = MIX // HEAD_DIM
D_FF = 2816
RW_COLS = 1824
RW_PAD = 1920
LORA_WA = 64
LORA_G = 160
S5_GROUPS = 32
S5_GROUP = 16
S5_STATE = 64
S5_LANES = S5_GROUPS * S5_STATE
QKVU_COLS = 4 * MIX + 128
PAGE = 128
NORM_EPS = 1e-6
LNX_EPS = 64e-5

LANE = 128
SEQ_ALIGN = 128
FF_TILE = 1408
FF_ROWS = 512
ATT_Q_TILE = 512
RWKV_CHUNK = 64
S5_CHUNK = 64
S5_SLAB_PAD = 8
SUBLANE = 8
DECODE_PAGES_PER_STEP = 16
VMEM_LIMIT = 56 * 1024 * 1024

NN = (((1,), (0,)), ((), ()))
NT = (((1,), (1,)), ((), ()))
TN = (((0,), (0,)), ((), ()))


def _params(sem):
    return pltpu.CompilerParams(dimension_semantics=sem, vmem_limit_bytes=VMEM_LIMIT)


def _tile(n, target):
    best = None
    for t in range(8, min(n, target) + 1, 8):
        if n % t == 0:
            best = t
    return best or n


def _dot(a, b, dn=NN):
    return lax.dot_general(a, b, dn, preferred_element_type=F32)


def _split2(x):
    hi = x.astype(BF16)
    return hi, (x - hi.astype(F32)).astype(BF16)


def _dot1(a, b, dn=NN):
    return _dot(a.astype(BF16), b.astype(BF16), dn)


def _dot_exact_lhs(a01, b):
    a = a01.astype(BF16)
    b1 = b.astype(BF16)
    r1 = b - b1.astype(F32)
    b2 = r1.astype(BF16)
    b3 = (r1 - b2.astype(F32)).astype(BF16)
    return _dot(a, b1) + _dot(a, b2) + _dot(a, b3)


def _dot_exact_rhs(a, b01):
    b = b01.astype(BF16)
    a1 = a.astype(BF16)
    r1 = a - a1.astype(F32)
    a2 = r1.astype(BF16)
    a3 = (r1 - a2.astype(F32)).astype(BF16)
    return _dot(a1, b) + _dot(a2, b) + _dot(a3, b)


def _segsum(x, ones_bd):
    hi, lo = _split2(x)
    return _dot(hi, ones_bd) + _dot(lo, ones_bd)


def _softplus(x):
    return jnp.maximum(x, 0.0) + jnp.log1p(jnp.exp(-jnp.abs(x)))


def _rms(x, g):
    ms = jnp.mean(x * x, axis=-1, keepdims=True)
    return x * lax.rsqrt(ms + NORM_EPS) * g


def _ffn_kernel(x_ref, g_ref, wg_ref, wu_ref, wo_ref, gn_ref, *rest, emit_hn):
    if emit_hn:
        o_ref, hn_ref, xn_s, acc_s = rest
    else:
        o_ref, xn_s, acc_s = rest
    j = pl.program_id(1)

    @pl.when(j == 0)
    def _():
        xn_s[...] = _rms(x_ref[...], g_ref[...]).astype(BF16)
        acc_s[...] = jnp.zeros_like(acc_s)

    xn = xn_s[...]
    gate = _dot(xn, wg_ref[...])
    up = _dot(xn, wu_ref[...])
    act = (gate * jax.nn.sigmoid(gate) * up).astype(BF16)
    acc_s[...] += _dot(act, wo_ref[...])

    @pl.when(j == pl.num_programs(1) - 1)
    def _():
        y = x_ref[...] + 0.5 * acc_s[...]
        o_ref[...] = y
        if emit_hn:
            hn_ref[...] = _rms(y, gn_ref[...]).astype(BF16)


def _ffn(x, g, w_in, w_out, g_next, emit_hn):
    rows = x.shape[0]
    tm = _tile(rows, FF_ROWS)
    nf = D_FF // FF_TILE
    out_shape = [jax.ShapeDtypeStruct((rows, D_MODEL), F32)]
    out_specs = [pl.BlockSpec((tm, D_MODEL), lambda i, j: (i, 0))]
    if emit_hn:
        out_shape.append(jax.ShapeDtypeStruct((rows, D_MODEL), BF16))
        out_specs.append(pl.BlockSpec((tm, D_MODEL), lambda i, j: (i, 0)))
    res = pl.pallas_call(
        functools.partial(_ffn_kernel, emit_hn=emit_hn),
        grid=(rows // tm, nf),
        in_specs=[
            pl.BlockSpec((tm, D_MODEL), lambda i, j: (i, 0)),
            pl.BlockSpec((1, D_MODEL), lambda i, j: (0, 0)),
            pl.BlockSpec((D_MODEL, FF_TILE), lambda i, j: (0, j)),
            pl.BlockSpec((D_MODEL, FF_TILE), lambda i, j: (0, j + nf)),
            pl.BlockSpec((FF_TILE, D_MODEL), lambda i, j: (j, 0)),
            pl.BlockSpec((1, D_MODEL), lambda i, j: (0, 0)),
        ],
        out_specs=out_specs,
        out_shape=out_shape,
        scratch_shapes=[pltpu.VMEM((tm, D_MODEL), BF16), pltpu.VMEM((tm, D_MODEL), F32)],
        compiler_params=_params(("parallel", "arbitrary")),
    )(x, g, w_in, w_in, w_out, g_next)
    return res if emit_hn else (res[0], None)


def _matmul_kernel(a_ref, b_ref, o_ref):
    o_ref[...] = _dot(a_ref[...], b_ref[...])


def _matmul(a, b, tn):
    rows, k = a.shape
    n = b.shape[1]
    tm = _tile(rows, 1024)
    return pl.pallas_call(
        _matmul_kernel,
        grid=(rows // tm, n // tn),
        in_specs=[pl.BlockSpec((tm, k), lambda i, j: (i, 0)), pl.BlockSpec((k, tn), lambda i, j: (0, j))],
        out_specs=pl.BlockSpec((tm, tn), lambda i, j: (i, j)),
        out_shape=jax.ShapeDtypeStruct((rows, n), F32),
        compiler_params=_params(("parallel", "arbitrary")),
    )(a, b)


def _qk_prep_kernel(q_ref, k_ref, f_ref, qg_ref, kg_ref, bf_ref, ones_ref, qn_ref, kn_ref, lf_ref, lfp_ref):
    ones_bd = ones_ref[...]

    def headnorm(x, g):
        ms = _segsum(x * x, ones_bd) * (1.0 / HEAD_DIM)
        return x * lax.rsqrt(ms + NORM_EPS) * g

    qn_ref[...] = (headnorm(q_ref[...], qg_ref[...]) * (HEAD_DIM ** -0.5)).astype(BF16)
    kn_ref[...] = headnorm(k_ref[...], kg_ref[...])
    lf = -_softplus(-(f_ref[...] + bf_ref[...]))
    lf_ref[...] = lf
    lane = lax.broadcasted_iota(jnp.int32, lf.shape, 1)
    for hp in range(N_HEADS // 2):
        a = lf[:, 2 * hp:2 * hp + 1]
        b = lf[:, 2 * hp + 1:2 * hp + 2]
        lfp_ref[:, hp * LANE:(hp + 1) * LANE] = jnp.where(lane == 0, a, jnp.where(lane == 1, b, 0.0))


def _qk_prep(qkvu, qg, kg, bf, ones_bd):
    rows = qkvu.shape[0]
    tm = _tile(rows, 1024)
    small = lambda w: pl.BlockSpec((1, w), lambda i: (0, 0))
    return pl.pallas_call(
        _qk_prep_kernel,
        grid=(rows // tm,),
        in_specs=[
            pl.BlockSpec((tm, MIX), lambda i: (i, 0)),
            pl.BlockSpec((tm, MIX), lambda i: (i, 1)),
            pl.BlockSpec((tm, LANE), lambda i: (i, 4 * MIX // LANE)),
            small(MIX), small(MIX), small(LANE),
            pl.BlockSpec((MIX, MIX), lambda i: (0, 0)),
        ],
        out_specs=[
            pl.BlockSpec((tm, MIX), lambda i: (i, 0)),
            pl.BlockSpec((tm, MIX), lambda i: (i, 0)),
            pl.BlockSpec((tm, LANE), lambda i: (i, 0)),
            pl.BlockSpec((tm, MIX), lambda i: (i, 0)),
        ],
        out_shape=[
            jax.ShapeDtypeStruct((rows, MIX), BF16),
            jax.ShapeDtypeStruct((rows, MIX), F32),
            jax.ShapeDtypeStruct((rows, LANE), F32),
            jax.ShapeDtypeStruct((rows, MIX), F32),
        ],
        compiler_params=_params(("parallel",)),
    )(qkvu, qkvu, qkvu, qg, kg, bf, ones_bd)


def _fox_attn_kernel(q_ref, k_ref, v_ref, lf_ref, o_ref, c_s):
    tp = q_ref.shape[0]
    cb = LANE
    row = lax.broadcasted_iota(jnp.int32, (cb, cb), 0)
    col = lax.broadcasted_iota(jnp.int32, (cb, cb), 1)
    tri = (row >= col).astype(F32)
    carry = jnp.zeros((1, LANE), F32)
    for blk in range(tp // cb):
        cblk = _dot_exact_lhs(tri, lf_ref[blk * cb:(blk + 1) * cb, :]) + carry
        c_s[blk * cb:(blk + 1) * cb, :] = cblk
        carry = cblk[cb - 1:cb, :]
    c = c_s[...]
    c_t = c.T
    q = q_ref[...]
    k = k_ref[...].astype(BF16)
    v = v_ref[...].astype(BF16)
    lane = lax.broadcasted_iota(jnp.int32, (1, LANE), 1)
    for r0 in range(0, tp, ATT_Q_TILE):
        r1 = min(r0 + ATT_Q_TILE, tp)
        tq = r1 - r0
        qi = lax.broadcasted_iota(jnp.int32, (tq, r1), 0) + r0
        ki = lax.broadcasted_iota(jnp.int32, (tq, r1), 1)
        causal = ki <= qi
        outs = []
        for e in range(2):
            head = (lane // HEAD_DIM) == e
            qe = jnp.where(head, q[r0:r1, :], jnp.zeros((), BF16))
            s = _dot(qe, k[:r1, :], NT)
            s = s + c[r0:r1, e:e + 1] - c_t[e:e + 1, :r1]
            s = jnp.where(causal, s, -1e30)
            m = jnp.max(s, axis=-1, keepdims=True)
            p = jnp.exp(s - m)
            l = jnp.sum(p, axis=-1, keepdims=True)
            outs.append(_dot(p.astype(BF16), v[:r1, :]) / l)
        o_ref[r0:r1, :] = jnp.where((lane // HEAD_DIM) == 0, outs[0], outs[1])


def _fox_attention(qn, kn, qkvu, lfp, bsz, tp):
    hp = N_HEADS // 2
    out = pl.pallas_call(
        _fox_attn_kernel,
        grid=(bsz, hp),
        in_specs=[
            pl.BlockSpec((None, tp, LANE), lambda b, h: (b, 0, h)),
            pl.BlockSpec((None, tp, LANE), lambda b, h: (b, 0, h)),
            pl.BlockSpec((None, tp, LANE), lambda b, h: (b, 0, 2 * MIX // LANE + h)),
            pl.BlockSpec((None, tp, LANE), lambda b, h: (b, 0, h)),
        ],
        out_specs=pl.BlockSpec((None, tp, LANE), lambda b, h: (b, 0, h)),
        out_shape=jax.ShapeDtypeStruct((bsz, tp, MIX), F32),
        scratch_shapes=[pltpu.VMEM((tp, LANE), F32)],
        compiler_params=_params(("parallel", "parallel")),
    )(qn.reshape(bsz, tp, MIX), kn.reshape(bsz, tp, MIX), qkvu.reshape(bsz, tp, QKVU_COLS),
      lfp.reshape(bsz, tp, MIX))
    return out.reshape(bsz * tp, MIX)


def _fox_decode_kernel(pt_ref, q_ref, kn_ref, vn_ref, lfn_ref, *rest, pages_per_step):
    del pt_ref
    g = pages_per_step
    k_refs, v_refs, lf_refs = rest[:g], rest[g:2 * g], rest[2 * g:3 * g]
    o_ref, m_s, l_s, acc_s, carry_s = rest[3 * g:]
    p = pl.program_id(1)
    hrow = lax.broadcasted_iota(jnp.int32, (N_HEADS, MIX), 0)
    hlane = lax.broadcasted_iota(jnp.int32, (N_HEADS, MIX), 1) // HEAD_DIM
    headmask = hrow == hlane
    qb = jnp.where(headmask, q_ref[...], 0.0)

    @pl.when(p == 0)
    def _():
        m_s[...] = jnp.sum(qb * kn_ref[...], axis=1, keepdims=True)
        l_s[...] = jnp.ones_like(l_s)
        acc_s[...] = jnp.where(headmask, vn_ref[...], 0.0)
        carry_s[...] = lfn_ref[...]

    srow = lax.broadcasted_iota(jnp.int32, (PAGE, PAGE), 0)
    scol = lax.broadcasted_iota(jnp.int32, (PAGE, PAGE), 1)
    later = (srow > scol).astype(F32)
    qbb = qb.astype(BF16)
    carry = carry_s[...]
    logits = []
    for i in range(g):
        lf = lf_refs[i][...]
        kt = k_refs[i][...].reshape(MIX, PAGE).astype(BF16)
        logits.append(_dot(qbb, kt) + (carry + _dot_exact_rhs(lf, later)))
        carry = carry + jnp.sum(lf, axis=1, keepdims=True)
    carry_s[...] = carry
    m_old = m_s[...]
    m_new = m_old
    for s in logits:
        m_new = jnp.maximum(m_new, jnp.max(s, axis=1, keepdims=True))
    alpha = jnp.exp(m_old - m_new)
    l = l_s[...] * alpha
    acc = acc_s[...] * alpha
    for i in range(g):
        pr = jnp.exp(logits[i] - m_new)
        l = l + jnp.sum(pr, axis=1, keepdims=True)
        acc = acc + _dot(pr.astype(BF16), v_refs[i][...].reshape(MIX, PAGE).astype(BF16), NT)
    m_s[...], l_s[...], acc_s[...] = m_new, l, acc

    @pl.when(p == pl.num_programs(1) - 1)
    def _():
        o_ref[...] = jnp.sum(jnp.where(headmask, acc / l, 0.0), axis=0, keepdims=True)


def _fox_decode(layer, q, kn, vn, lfn, cache_kt, cache_vt, cache_lft, page_table):
    bsz, n_pages = page_table.shape
    g = DECODE_PAGES_PER_STEP
    assert n_pages % g == 0

    def page_spec(block, i):
        return pl.BlockSpec((None, None) + block,
                            lambda b, p, pt: (layer, pt[b, n_pages - 1 - (p * g + i)]) + (0,) * len(block))

    tok = lambda r, w: pl.BlockSpec((None, r, w), lambda b, p, pt: (b, 0, 0))
    in_specs = [tok(1, MIX), tok(1, MIX), tok(1, MIX), tok(N_HEADS, 1)]
    in_specs += [page_spec((N_HEADS, HEAD_DIM, PAGE), i) for i in range(g)]
    in_specs += [page_spec((N_HEADS, HEAD_DIM, PAGE), i) for i in range(g)]
    in_specs += [page_spec((N_HEADS, PAGE), i) for i in range(g)]
    grid_spec = pltpu.PrefetchScalarGridSpec(
        num_scalar_prefetch=1,
        grid=(bsz, n_pages // g),
        in_specs=in_specs,
        out_specs=tok(1, MIX),
        scratch_shapes=[pltpu.VMEM((N_HEADS, 1), F32), pltpu.VMEM((N_HEADS, 1), F32),
                        pltpu.VMEM((N_HEADS, MIX), F32), pltpu.VMEM((N_HEADS, 1), F32)],
    )
    return pl.pallas_call(
        functools.partial(_fox_decode_kernel, pages_per_step=g),
        grid_spec=grid_spec,
        out_shape=jax.ShapeDtypeStruct((bsz, 1, MIX), F32),
        compiler_params=_params(("parallel", "arbitrary")),
    )(page_table, q, kn, vn, lfn, *([cache_kt] * g), *([cache_vt] * g), *([cache_lft] * g))


def _rwkv_prep_kernel(rw_ref, prev_ref, sh0_ref, mu_ref, w0_ref, w2_ref, a0_ref, a2_ref, g2_ref,
                      kk_ref, ka_ref, rk_ref, ones_ref,
                      r_out, lw_out, k_out, v_out, kk_out, b_out, g_out, bonus_out):
    i = pl.program_id(1)
    rw = rw_ref[...]
    tm = rw.shape[0]
    first = jnp.where(i == 0, sh0_ref[...], prev_ref[SUBLANE - 1:SUBLANE, :])
    trow = lax.broadcasted_iota(jnp.int32, (tm, 1), 0)
    prev = jnp.where(trow == 0, first, pltpu.roll(rw, 1, 0))
    z = rw + (prev - rw) * mu_ref[...]
    r = z[:, 0:MIX]
    k = z[:, MIX:2 * MIX]
    v = z[:, 2 * MIX:3 * MIX]
    xwa = z[:, 3 * MIX:3 * MIX + 2 * LORA_WA]
    xg = z[:, 3 * MIX + 2 * LORA_WA:]
    ones_bd = ones_ref[...]
    w = -_softplus(-(w0_ref[...] + _dot(jnp.tanh(xwa).astype(BF16), w2_ref[...]))) - 0.5
    lw_out[...] = -jnp.exp(w)
    a = jax.nn.sigmoid(a0_ref[...] + _dot(xwa.astype(BF16), a2_ref[...]))
    g_out[...] = _dot(jax.nn.sigmoid(xg).astype(BF16), g2_ref[...])
    kk = k * kk_ref[...]
    kk = kk / jnp.maximum(jnp.sqrt(_segsum(kk * kk, ones_bd)), 1e-12)
    k2 = k * (1.0 + (a - 1.0) * ka_ref[...])
    r_out[...] = r.astype(BF16)
    k_out[...] = k2.astype(BF16)
    v_out[...] = v.astype(BF16)
    kk_out[...] = kk.astype(BF16)
    b_out[...] = (kk * a).astype(BF16)
    bonus_out[...] = _segsum(r * k2 * rk_ref[...], ones_bd) * v


def _rwkv_prep(rwp, shift0, prm, bsz, tp):
    tm = _tile(tp, 544)
    nb = tm // SUBLANE
    small = lambda a: pl.BlockSpec(a.shape, lambda b, i: (0, 0))
    consts = [prm["mu"], prm["w0"], prm["w2"], prm["a0"], prm["a2"], prm["g2"], prm["k_k"], prm["k_a"],
              prm["r_k"], prm["ones_bd"]]
    rw3 = rwp.reshape(bsz, tp, RW_PAD)
    outs = pl.pallas_call(
        _rwkv_prep_kernel,
        grid=(bsz, tp // tm),
        in_specs=[
            pl.BlockSpec((None, tm, RW_PAD), lambda b, i: (b, i, 0)),
            pl.BlockSpec((None, SUBLANE, RW_PAD), lambda b, i: (b, jnp.maximum(i * nb - 1, 0), 0)),
            pl.BlockSpec((None, 1, RW_PAD), lambda b, i: (b, 0, 0)),
        ] + [small(c) for c in consts],
        out_specs=[pl.BlockSpec((None, tm, MIX), lambda b, i: (b, i, 0))] * 8,
        out_shape=[jax.ShapeDtypeStruct((bsz, tp, MIX), dt) for dt in (BF16, F32, BF16, BF16, BF16, BF16, F32, F32)],
        compiler_params=_params(("parallel", "parallel")),
    )(rw3, rw3, shift0, *consts)
    return outs


def _rwkv_scan_kernel(r_ref, lw_ref, k_ref, v_ref, kk_ref, b_ref, s0_ref, o_ref, s_out, s_s, *, chunk, t_valid):
    c = pl.program_id(1)
    n = chunk

    @pl.when(c == 0)
    def _():
        s_s[...] = s0_ref[...]

    trow = lax.broadcasted_iota(jnp.int32, (n, 1), 0) + c * n
    valid = trow < t_valid
    lw = jnp.where(valid, lw_ref[...], 0.0)
    kk = jnp.where(valid, kk_ref[...].astype(F32), 0.0)
    bb = jnp.where(valid, b_ref[...].astype(F32), 0.0)
    kx = jnp.where(valid, k_ref[...].astype(F32), 0.0)
    vx = jnp.where(valid, v_ref[...].astype(F32), 0.0)
    rx = r_ref[...].astype(F32)

    row = lax.broadcasted_iota(jnp.int32, (n, n), 0)
    col = lax.broadcasted_iota(jnp.int32, (n, n), 1)
    incl = row >= col
    strict = row > col
    eye = (row == col).astype(F32)
    ci = _dot_exact_lhs(incl.astype(F32), lw)
    ce = ci - lw
    kkd = kk * jnp.exp(ce)
    rd = rx * jnp.exp(ci)
    einv = jnp.exp(-ci)
    bd = bb * einv
    kd = kx * einv
    cl = ci[n - 1:n, :]
    wl = jnp.exp(cl)
    dl = jnp.exp(cl - ci)
    bdw = bb * dl
    kdw = kx * dl
    row2 = lax.broadcasted_iota(jnp.int32, (n, 2 * n), 0)
    col2 = lax.broadcasted_iota(jnp.int32, (n, 2 * n), 1)
    col2 = jnp.where(col2 >= n, col2 - n, col2)
    strict2 = row2 > col2
    incl2 = row2 >= col2

    hs = range(N_HEADS)
    sls = [slice(h * HEAD_DIM, (h + 1) * HEAD_DIM) for h in hs]
    s_all = [s_s[h] for h in hs]
    a_m = [jnp.concatenate([kkd[:, sl], rd[:, sl]], axis=0) for sl in sls]
    b_m = [jnp.concatenate([bd[:, sl], kd[:, sl]], axis=0) for sl in sls]
    pm =[_dot1(a_m[h], b_m[h], NT) for h in hs]
    p_top = [jnp.where(strict2, pm[h][:n, :], 0.0) for h in hs]
    p_bot = [jnp.where(incl2, pm[h][n:, :], 0.0) for h in hs]
    gm = [_dot1(a_m[h], s_all[h], NT) for h in hs]
    x = [-p_top[h][:, :n] for h in hs]
    tinv = [eye + x[h] for h in hs]
    span = 2
    while span < n:
        x = [_dot1(x[h], x[h]) for h in hs]
        tinv = [tinv[h] + _dot1(tinv[h], x[h]) for h in hs]
        span *= 2
    v_h = [vx[:, sl] for sl in sls]
    rhs = [gm[h][:n, :] + _dot1(p_top[h][:, n:], v_h[h]) for h in hs]
    u = [-_dot1(tinv[h], rhs[h]) for h in hs]
    uv = [jnp.concatenate([u[h], v_h[h]], axis=0) for h in hs]
    outs = [gm[h][n:, :] + _dot1(p_bot[h], uv[h]) for h in hs]
    bk = [jnp.concatenate([bdw[:, sl], kdw[:, sl]], axis=0) for sl in sls]
    s_new = [s_all[h] * wl[:, sls[h]] + _dot1(uv[h], bk[h], TN) for h in hs]
    for h in hs:
        s_s[h] = s_new[h]
    o_ref[...] = jnp.concatenate(outs, axis=1)

    @pl.when(c == pl.num_programs(1) - 1)
    def _():
        s_out[...] = s_s[...]


def _rwkv_scan(r, lw, k, v, kk, b, s0, bsz, tp, t_valid, chunk):
    blk = pl.BlockSpec((None, chunk, MIX), lambda bi, c: (bi, c, 0))
    st = pl.BlockSpec((None, N_HEADS, HEAD_DIM, HEAD_DIM), lambda bi, c: (bi, 0, 0, 0))
    o, s_out = pl.pallas_call(
        functools.partial(_rwkv_scan_kernel, chunk=chunk, t_valid=t_valid),
        grid=(bsz, tp // chunk),
        in_specs=[blk] * 6 + [st],
        out_specs=[blk, st],
        out_shape=[jax.ShapeDtypeStruct((bsz, tp, MIX), F32),
                   jax.ShapeDtypeStruct((bsz, N_HEADS, HEAD_DIM, HEAD_DIM), F32)],
        scratch_shapes=[pltpu.VMEM((N_HEADS, HEAD_DIM, HEAD_DIM), F32)],
        compiler_params=_params(("parallel", "arbitrary")),
    )(r, lw, k, v, kk, b, s0)
    return o.reshape(bsz * tp, MIX), s_out


def _s5_kernel(u_ref, x0_ref, a_ref, bm_ref, cm_ref, d_ref, gw_ref, gb_ref, o_ref, xo_ref, buf_s, st_s,
               *, t_valid):
    c = pl.program_id(0)
    bsz, tc, _ = u_ref.shape
    tcp = buf_s.shape[1] // bsz
    half = S5_LANES // 2
    hc = MIX // 2
    nblk = S5_LANES // LANE
    hblk = half // LANE

    @pl.when(c == 0)
    def _():
        st_s[...] = x0_ref[...]

    u2 = u_ref[...].reshape(bsz * tc, MIX)
    ub = u2.astype(BF16)
    for part in range(2):
        for j in range(2):
            col0 = part * S5_LANES + j * half
            res = _dot(ub[:, j * hc:(j + 1) * hc], bm_ref[j * hc:(j + 1) * hc, col0:col0 + half])
            for k in range(hblk):
                for b in range(bsz):
                    buf_s[col0 // LANE + k, b * tcp:b * tcp + tc, :] = res[b * tc:(b + 1) * tc,
                                                                           k * LANE:(k + 1) * LANE]

    gb = max(1, 2 * nblk // bsz)
    for g0 in range(0, nblk, gb):
        blks = list(range(g0, g0 + gb))
        lane = lambda k: slice(k * LANE, (k + 1) * LANE)
        ar = [jnp.broadcast_to(a_ref[:, lane(k)], (bsz, LANE)) for k in blks]
        ai = [jnp.broadcast_to(a_ref[:, lane(nblk + k)], (bsz, LANE)) for k in blks]

        def body(t, carry, blks=blks, ar=ar, ai=ai):
            rows = pl.ds(t, bsz, stride=tcp)
            out = []
            for n, k in enumerate(blks):
                xr, xi = carry[n]
                nxr = ar[n] * xr - ai[n] * xi + buf_s[k, rows, :]
                nxi = ar[n] * xi + ai[n] * xr + buf_s[nblk + k, rows, :]
                buf_s[k, rows, :] = nxr
                buf_s[nblk + k, rows, :] = nxi
                out.append((nxr, nxi))
            return tuple(out)

        init = tuple((st_s[:, lane(k)], st_s[:, lane(nblk + k)]) for k in blks)
        fin = lax.fori_loop(0, tc, body, init)
        for n, k in enumerate(blks):
            st_s[:, lane(k)] = fin[n][0]
            st_s[:, lane(nblk + k)] = fin[n][1]

    c_valid, t_in = divmod(t_valid - 1, tc)

    @pl.when(c == c_valid)
    def _():
        for k in range(2 * nblk):
            xo_ref[:, k * LANE:(k + 1) * LANE] = buf_s[k, pl.ds(t_in, bsz, stride=tcp), :]

    ys = []
    for j in range(2):
        acc = None
        for part in range(2):
            col0 = part * S5_LANES + j * half
            xs = jnp.concatenate(
                [jnp.concatenate([buf_s[col0 // LANE + k, b * tcp:b * tcp + tc, :] for b in range(bsz)], axis=0)
                 for k in range(hblk)], axis=1)
            d = _dot(xs.astype(BF16), cm_ref[col0:col0 + half, j * hc:(j + 1) * hc])
            acc = d if acc is None else acc + d
        ys.append(acc)
    y = jnp.concatenate(ys, axis=1) + d_ref[...] * u2
    y = 0.5 * y * (1.0 + jnp.tanh(0.7978845608028654 * (y + 0.044715 * (y * y * y))))
    gate = jax.nn.sigmoid(_dot(y.astype(BF16), gw_ref[...]) + gb_ref[...])
    o_ref[...] = (y * gate).reshape(bsz, tc, MIX)


def _s5(qkvu, x0, prm, bsz, tp, t_valid, tc):
    n2 = 2 * S5_LANES
    full = lambda a: pl.BlockSpec(a.shape, lambda c: (0,) * a.ndim)
    consts = [prm["s5_a"], prm["s5_bm"], prm["s5_cm"], prm["s5_d"], prm["s5_gw"], prm["s5_gb"]]
    o, xo = pl.pallas_call(
        functools.partial(_s5_kernel, t_valid=t_valid),
        grid=(tp // tc,),
        in_specs=[pl.BlockSpec((bsz, tc, MIX), lambda c: (0, c, 3)), full(x0)] + [full(a) for a in consts],
        out_specs=[pl.BlockSpec((bsz, tc, MIX), lambda c: (0, c, 0)), pl.BlockSpec((bsz, n2), lambda c: (0, 0))],
        out_shape=[jax.ShapeDtypeStruct((bsz, tp, MIX), F32), jax.ShapeDtypeStruct((bsz, n2), F32)],
        scratch_shapes=[pltpu.VMEM((n2 // LANE, bsz * (tc + S5_SLAB_PAD), LANE), F32),
                        pltpu.VMEM((bsz, n2), F32)],
        compiler_params=_params(("arbitrary",)),
    )(qkvu.reshape(bsz, tp, QKVU_COLS), x0, *consts)
    return o.reshape(bsz * tp, MIX), xo


def _merge_kernel(x_ref, oa_ref, ob_ref, g_ref, bonus_ref, oc_ref, gates_ref, lw_ref, lb_ref, ones_ref,
                  wb_ref, wo_ref, o_ref):
    ones_bd = ones_ref[...]
    o = ob_ref[...]
    mu = _segsum(o, ones_bd) * (1.0 / HEAD_DIM)
    dlt = o - mu
    var = _segsum(dlt * dlt, ones_bd) * (1.0 / HEAD_DIM)
    ob = (dlt * lax.rsqrt(var + LNX_EPS) * lw_ref[...] + lb_ref[...] + bonus_ref[...]) * g_ref[...]
    merged = jax.nn.sigmoid(gates_ref[:, 0:D_MODEL]) * _dot(oa_ref[...].astype(BF16), wb_ref[0:MIX, :])
    merged += jax.nn.sigmoid(gates_ref[:, D_MODEL:2 * D_MODEL]) * _dot(ob.astype(BF16), wb_ref[MIX:2 * MIX, :])
    merged += jax.nn.sigmoid(gates_ref[:, 2 * D_MODEL:]) * _dot(oc_ref[...].astype(BF16), wb_ref[2 * MIX:, :])
    o_ref[...] = x_ref[...] + _dot(merged.astype(BF16), wo_ref[...])


def _merge(x, oa, ob, g, bonus, oc, gates, prm):
    rows = x.shape[0]
    tm = _tile(rows, 512)
    row = lambda w: pl.BlockSpec((tm, w), lambda i: (i, 0))
    full = lambda a: pl.BlockSpec(a.shape, lambda i: (0, 0))
    consts = [prm["lnx_w"], prm["lnx_b"], prm["ones_bd"], prm["w_branch"], prm["w_out"]]
    return pl.pallas_call(
        _merge_kernel,
        grid=(rows // tm,),
        in_specs=[row(D_MODEL), row(MIX), row(MIX), row(MIX), row(MIX), row(MIX), row(3 * D_MODEL)]
        + [full(c) for c in consts],
        out_specs=row(D_MODEL),
        out_shape=jax.ShapeDtypeStruct((rows, D_MODEL), F32),
        compiler_params=_params(("parallel",)),
    )(x, oa, ob, g, bonus, oc, gates, *consts)


def _layer_params(l, p):
    bf = lambda a: a.astype(BF16)
    row = lambda a, w=None: (a if w is None else jnp.pad(a, (0, w - a.shape[0]))).reshape(1, -1).astype(F32)
    w_in = p["w_in"][l]
    o_f = 3 * MIX
    o_rw = o_f + N_HEADS
    o_u = o_rw + RW_COLS
    o_g = o_u + MIX
    w_qkvu = jnp.concatenate(
        [w_in[:, :o_f], w_in[:, o_u:o_g], jnp.pad(w_in[:, o_f:o_rw], ((0, 0), (0, LANE - N_HEADS)))], axis=1)
    w_rw = jnp.pad(w_in[:, o_rw:o_u], ((0, 0), (0, RW_PAD - RW_COLS)))
    w_gates = w_in[:, o_g:]

    ar = p["s5_a_re"][l].astype(F32)
    ai = p["s5_a_im"][l].astype(F32)
    dt = jnp.exp(p["s5_log_dt"][l].astype(F32))[:, None]
    mag = jnp.exp(dt * ar)
    abar_re = mag * jnp.cos(dt * ai)
    abar_im = mag * jnp.sin(dt * ai)
    den = ar * ar + ai * ai
    nr, ni = abar_re - 1.0, abar_im
    z_re = (nr * ar + ni * ai) / den
    z_im = (ni * ar - nr * ai) / den
    b_re = p["s5_b_re"][l].astype(F32)
    b_im = p["s5_b_im"][l].astype(F32)
    bb_re = z_re[..., None] * b_re - z_im[..., None] * b_im
    bb_im = z_re[..., None] * b_im + z_im[..., None] * b_re
    eye = jnp.eye(S5_GROUPS, dtype=F32)
    blk_in = lambda m: jnp.einsum("gpc,gh->gchp", m, eye).reshape(MIX, S5_LANES)
    blk_out = lambda m: jnp.einsum("gcp,gh->gphc", m, eye).reshape(S5_LANES, MIX)
    s5_bm = jnp.concatenate([blk_in(bb_re), blk_in(bb_im)], axis=1)
    s5_cm = jnp.concatenate([blk_out(p["s5_c_re"][l].astype(F32)), -blk_out(p["s5_c_im"][l].astype(F32))], axis=0)

    hd = lax.broadcasted_iota(jnp.int32, (MIX, MIX), 0) // HEAD_DIM
    hd_t = lax.broadcasted_iota(jnp.int32, (MIX, MIX), 1) // HEAD_DIM
    return dict(
        ffn1_norm=row(p["ffn1_norm"][l]), ffn1_w_in=bf(p["ffn1_w_in"][l]), ffn1_w_out=bf(p["ffn1_w_out"][l]),
        ffn2_norm=row(p["ffn2_norm"][l]), ffn2_w_in=bf(p["ffn2_w_in"][l]), ffn2_w_out=bf(p["ffn2_w_out"][l]),
        mix_norm=row(p["mix_norm"][l]),
        w_qkvu=bf(w_qkvu), w_rw=bf(w_rw), w_gates=bf(w_gates),
        q_gain=row(jnp.tile(p["fox_q_gain"][l], N_HEADS)), k_gain=row(jnp.tile(p["fox_k_gain"][l], N_HEADS)),
        b_f=row(p["fox_b_f"][l], LANE),
        ones_bd=(hd == hd_t).astype(BF16),
        mu=row(p["rwkv_mu"][l], RW_PAD), w0=row(p["rwkv_w0"][l]), a0=row(p["rwkv_a0"][l]),
        w2=bf(jnp.pad(p["rwkv_w2"][l], ((0, LORA_WA), (0, 0)))),
        a2=bf(jnp.pad(p["rwkv_a2"][l], ((LORA_WA, 0), (0, 0)))),
        g2=bf(jnp.pad(p["rwkv_g2"][l], ((0, RW_PAD - RW_COLS), (0, 0)))),
        k_k=row(p["rwkv_k_k"][l]), k_a=row(p["rwkv_k_a"][l]), r_k=row(p["rwkv_r_k"][l].reshape(-1)),
        lnx_w=row(p["rwkv_lnx_w"][l]), lnx_b=row(p["rwkv_lnx_b"][l]),
        s5_a=jnp.concatenate([abar_re.reshape(1, -1), abar_im.reshape(1, -1)], axis=1),
        s5_bm=bf(s5_bm), s5_cm=bf(s5_cm), s5_d=row(p["s5_d"][l].reshape(-1)),
        s5_gw=bf(p["s5_glu_w"][l]), s5_gb=row(p["s5_glu_b"][l]),
        w_branch=bf(p["w_branch"][l]), w_out=bf(p["w_out"][l]),
    )


def _mixer_inputs(x, prm):
    x, hn = _ffn(x, prm["ffn1_norm"], prm["ffn1_w_in"], prm["ffn1_w_out"], prm["mix_norm"], True)
    qkvu = _matmul(hn, prm["w_qkvu"], QKVU_COLS)
    rwp = _matmul(hn, prm["w_rw"], RW_PAD)
    gates = _matmul(hn, prm["w_gates"], D_MODEL)
    qn, kn, lf, lfp = _qk_prep(qkvu, prm["q_gain"], prm["k_gain"], prm["b_f"], prm["ones_bd"])
    return x, qkvu, rwp, gates, qn, kn, lf, lfp


def _layer_tail(x, oa, qkvu, rwp, gates, prm, shift0, s0, x0, bsz, tp, t_valid, chunk, tc):
    r, lw, k2, v, kk, b, g, bonus = _rwkv_prep(rwp, shift0, prm, bsz, tp)
    ob, s_out = _rwkv_scan(r, lw, k2, v, kk, b, s0, bsz, tp, t_valid, chunk)
    oc, xo = _s5(qkvu, x0, prm, bsz, tp, t_valid, tc)
    flat = lambda a: a.reshape(bsz * tp, MIX)
    x = _merge(x, oa, ob, flat(g), flat(bonus), oc, gates, prm)
    x, _ = _ffn(x, prm["ffn2_norm"], prm["ffn2_w_in"], prm["ffn2_w_out"], prm["ffn2_norm"], False)
    return x, s_out, xo


def kernel(x_prompt, x_sample, cache_k, cache_v, cache_logf, page_table, state_rwkv, state_shift, state_s5_re, state_s5_im, meta_tokens, ffn1_norm, ffn1_w_in, ffn1_w_out, mix_norm, w_in, fox_b_f, fox_q_gain, fox_k_gain, rwkv_mu, rwkv_w0, rwkv_w2, rwkv_a0, rwkv_a2, rwkv_g2, rwkv_k_k, rwkv_k_a, rwkv_r_k, rwkv_lnx_w, rwkv_lnx_b, s5_a_re, s5_a_im, s5_log_dt, s5_b_re, s5_b_im, s5_c_re, s5_c_im, s5_d, s5_glu_w, s5_glu_b, w_branch, w_out, ffn2_norm, ffn2_w_in, ffn2_w_out):
    p = dict(ffn1_norm=ffn1_norm, ffn1_w_in=ffn1_w_in, ffn1_w_out=ffn1_w_out, mix_norm=mix_norm,
             w_in=w_in, fox_b_f=fox_b_f, fox_q_gain=fox_q_gain, fox_k_gain=fox_k_gain,
             rwkv_mu=rwkv_mu, rwkv_w0=rwkv_w0, rwkv_w2=rwkv_w2, rwkv_a0=rwkv_a0, rwkv_a2=rwkv_a2,
             rwkv_g2=rwkv_g2, rwkv_k_k=rwkv_k_k, rwkv_k_a=rwkv_k_a, rwkv_r_k=rwkv_r_k,
             rwkv_lnx_w=rwkv_lnx_w, rwkv_lnx_b=rwkv_lnx_b, s5_a_re=s5_a_re, s5_a_im=s5_a_im,
             s5_log_dt=s5_log_dt, s5_b_re=s5_b_re, s5_b_im=s5_b_im, s5_c_re=s5_c_re, s5_c_im=s5_c_im,
             s5_d=s5_d, s5_glu_w=s5_glu_w, s5_glu_b=s5_glu_b, w_branch=w_branch, w_out=w_out,
             ffn2_norm=ffn2_norm, ffn2_w_in=ffn2_w_in, ffn2_w_out=ffn2_w_out)
    depth = w_in.shape[0]
    prms = [_layer_params(l, p) for l in range(depth)]

    bp, seq, _ = x_prompt.shape
    t_p = seq + N_META
    tp = -(-t_p // SEQ_ALIGN) * SEQ_ALIGN
    meta = jnp.broadcast_to(meta_tokens.astype(F32)[None], (bp, N_META, D_MODEL))
    xp = jnp.concatenate([meta, x_prompt], axis=1)
    xp = jnp.pad(xp, ((0, 0), (0, tp - t_p), (0, 0))).reshape(bp * tp, D_MODEL)
    z_shift = jnp.zeros((bp, 1, RW_PAD), F32)
    z_s = jnp.zeros((bp, N_HEADS, HEAD_DIM, HEAD_DIM), F32)
    z_x = jnp.zeros((bp, 2 * S5_LANES), F32)
    seqs = lambda a, w: a.reshape(bp, tp, w)
    pk, pv, plf, prw, psh, pre, pim = [], [], [], [], [], [], []
    for l in range(depth):
        prm = prms[l]
        xp, qkvu, rwp, gates, qn, kn, lf, lfp = _mixer_inputs(xp, prm)
        oa = _fox_attention(qn, kn, qkvu, lfp, bp, tp)
        xp, s_out, xo = _layer_tail(xp, oa, qkvu, rwp, gates, prm, z_shift, z_s, z_x, bp, tp, t_p,
                                    RWKV_CHUNK, S5_CHUNK)
        pk.append(seqs(kn, MIX)[:, :t_p].reshape(bp, t_p, N_HEADS, HEAD_DIM))
        pv.append(seqs(qkvu, QKVU_COLS)[:, :t_p, 2 * MIX:3 * MIX].reshape(bp, t_p, N_HEADS, HEAD_DIM))
        plf.append(seqs(lf, LANE)[:, :t_p, :N_HEADS])
        prw.append(s_out)
        psh.append(seqs(rwp, RW_PAD)[:, t_p - 1, :RW_COLS])
        pre.append(xo[:, :S5_LANES].reshape(bp, S5_GROUPS, S5_STATE))
        pim.append(xo[:, S5_LANES:].reshape(bp, S5_GROUPS, S5_STATE))
    y_prompt = seqs(xp, D_MODEL)[:, N_META:t_p]

    bs = x_sample.shape[0]
    ts = SUBLANE
    xs = jnp.pad(x_sample, ((0, 0), (0, ts - 1), (0, 0))).reshape(bs * ts, D_MODEL)
    ck =cache_k.transpose(0, 1, 3, 4, 2)
    cv = cache_v.transpose(0, 1, 3, 4, 2)
    clf = cache_logf.transpose(0, 1, 3, 2)
    tok0 = lambda a, w: a.reshape(bs, ts, w)[:, 0]
    sk, sv, slf, srw, ssh, sre, sim = [], [], [], [], [], [], []
    for l in range(depth):
        prm = prms[l]
        xs, qkvu, rwp, gates, qn, kn, lf, lfp = _mixer_inputs(xs, prm)
        q_new = tok0(qn, MIX).astype(F32)
        k_new = tok0(kn, MIX)
        v_new = tok0(qkvu, QKVU_COLS)[:, 2 * MIX:3 * MIX]
        lf_new = tok0(lf, LANE)[:, :N_HEADS]
        oa = _fox_decode(l, q_new[:, None], k_new[:, None], v_new[:, None], lf_new[:, :, None], ck, cv, clf,
                         page_table)
        oa = jnp.pad(oa, ((0, 0), (0, ts - 1), (0, 0))).reshape(bs * ts, MIX)
        shift0 = jnp.pad(state_shift[l].astype(F32), ((0, 0), (0, RW_PAD - RW_COLS)))[:, None]
        x0 = jnp.concatenate([state_s5_re[l].reshape(bs, S5_LANES), state_s5_im[l].reshape(bs, S5_LANES)],
                             axis=1).astype(F32)
        xs, s_out, xo = _layer_tail(xs, oa, qkvu, rwp, gates, prm, shift0, state_rwkv[l].astype(F32), x0,
                                    bs, ts, 1, ts, ts)
        sk.append(k_new.reshape(bs, 1, N_HEADS, HEAD_DIM))
        sv.append(v_new.reshape(bs, 1, N_HEADS, HEAD_DIM))
        slf.append(lf_new.reshape(bs, 1, N_HEADS))
        srw.append(s_out)
        ssh.append(tok0(rwp, RW_PAD)[:, :RW_COLS])
        sre.append(xo[:, :S5_LANES].reshape(bs, S5_GROUPS, S5_STATE))
        sim.append(xo[:, S5_LANES:].reshape(bs, S5_GROUPS, S5_STATE))
    y_sample = tok0(xs, D_MODEL).reshape(bs, 1, D_MODEL)

    st = lambda xs_: jnp.stack(xs_, axis=0)
    return (y_prompt, y_sample, st(pk), st(pv), st(plf), st(prw), st(psh), st(pre), st(pim),
            st(sk), st(sv), st(slf), st(srw), st(ssh), st(sre), st(sim))
```

```python
import functools

import jax
import jax.numpy as jnp
from jax import lax
from jax.experimental import pallas as pl
from jax.experimental.pallas import tpu as pltpu

F32 = jnp.float32
BF16 = jnp.bfloat16

D_MODEL = 1024
N_META = 16
HEAD_DIM = 64
MIX = D_MODEL // 2
N_HEADS = MIX // HEAD_DIM
D_FF = 2816
RW_COLS = 1824
RW_PAD = 1920
LORA_WA = 64
LORA_G = 160
S5_GROUPS = 32
S5_GROUP = 16
S5_STATE = 64
S5_LANES = S5_GROUPS * S5_STATE
QKVU_COLS = 4 * MIX + 128
PAGE = 128
NORM_EPS = 1e-6
LNX_EPS = 64e-5

LANE = 128
SEQ_ALIGN = 128
FF_TILE = 1408
FF_ROWS = 512
ATT_Q_TILE = 512
RWKV_CHUNK = 64
RWKV_SEQS_PER_STEP = 2
S5_CHUNK = 64
S5_SLAB_PAD = 8
SUBLANE = 8
DECODE_PAGES_PER_STEP = 16
VMEM_LIMIT = 56 * 1024 * 1024

NN = (((1,), (0,)), ((), ()))
NT = (((1,), (1,)), ((), ()))
TN = (((0,), (0,)), ((), ()))


def _params(sem):
    return pltpu.CompilerParams(dimension_semantics=sem, vmem_limit_bytes=VMEM_LIMIT)


def _tile(n, target):
    best = None
    for t in range(8, min(n, target) + 1, 8):
        if n % t == 0:
            best = t
    return best or n


def _dot(a, b, dn=NN):
    return lax.dot_general(a, b, dn, preferred_element_type=F32)


def _split2(x):
    hi = x.astype(BF16)
    return hi, (x - hi.astype(F32)).astype(BF16)


def _dot1(a, b, dn=NN):
    return _dot(a.astype(BF16), b.astype(BF16), dn)


def _dot_exact_lhs(a01, b):
    a = a01.astype(BF16)
    b1 = b.astype(BF16)
    r1 = b - b1.astype(F32)
    b2 = r1.astype(BF16)
    b3 = (r1 - b2.astype(F32)).astype(BF16)
    return _dot(a, b1) + _dot(a, b2) + _dot(a, b3)


def _dot_exact_rhs(a, b01):
    b = b01.astype(BF16)
    a1 = a.astype(BF16)
    r1 = a - a1.astype(F32)
    a2 = r1.astype(BF16)
    a3 = (r1 - a2.astype(F32)).astype(BF16)
    return _dot(a1, b) + _dot(a2, b) + _dot(a3, b)


def _segsum(x, ones_bd):
    hi, lo = _split2(x)
    return _dot(hi, ones_bd) + _dot(lo, ones_bd)


def _softplus(x):
    return jnp.maximum(x, 0.0) + jnp.log1p(jnp.exp(-jnp.abs(x)))


def _rms(x, g):
    ms = jnp.mean(x * x, axis=-1, keepdims=True)
    return x * lax.rsqrt(ms + NORM_EPS) * g


def _ffn_kernel(x_ref, g_ref, wg_ref, wu_ref, wo_ref, gn_ref, *rest, emit_hn):
    if emit_hn:
        o_ref, hn_ref, xn_s, acc_s = rest
    else:
        o_ref, xn_s, acc_s = rest
    j = pl.program_id(1)

    @pl.when(j == 0)
    def _():
        xn_s[...] = _rms(x_ref[...], g_ref[...]).astype(BF16)
        acc_s[...] = jnp.zeros_like(acc_s)

    xn = xn_s[...]
    gate = _dot(xn, wg_ref[...])
    up = _dot(xn, wu_ref[...])
    act = (gate * jax.nn.sigmoid(gate) * up).astype(BF16)
    acc_s[...] += _dot(act, wo_ref[...])

    @pl.when(j == pl.num_programs(1) - 1)
    def _():
        y = x_ref[...] + 0.5 * acc_s[...]
        o_ref[...] = y
        if emit_hn:
            hn_ref[...] = _rms(y, gn_ref[...]).astype(BF16)


def _ffn(x, g, w_in, w_out, g_next, emit_hn):
    rows = x.shape[0]
    tm = _tile(rows, FF_ROWS)
    nf = D_FF // FF_TILE
    out_shape = [jax.ShapeDtypeStruct((rows, D_MODEL), F32)]
    out_specs = [pl.BlockSpec((tm, D_MODEL), lambda i, j: (i, 0))]
    if emit_hn:
        out_shape.append(jax.ShapeDtypeStruct((rows, D_MODEL), BF16))
        out_specs.append(pl.BlockSpec((tm, D_MODEL), lambda i, j: (i, 0)))
    res = pl.pallas_call(
        functools.partial(_ffn_kernel, emit_hn=emit_hn),
        grid=(rows // tm, nf),
        in_specs=[
            pl.BlockSpec((tm, D_MODEL), lambda i, j: (i, 0)),
            pl.BlockSpec((1, D_MODEL), lambda i, j: (0, 0)),
            pl.BlockSpec((D_MODEL, FF_TILE), lambda i, j: (0, j)),
            pl.BlockSpec((D_MODEL, FF_TILE), lambda i, j: (0, j + nf)),
            pl.BlockSpec((FF_TILE, D_MODEL), lambda i, j: (j, 0)),
            pl.BlockSpec((1, D_MODEL), lambda i, j: (0, 0)),
        ],
        out_specs=out_specs,
        out_shape=out_shape,
        scratch_shapes=[pltpu.VMEM((tm, D_MODEL), BF16), pltpu.VMEM((tm, D_MODEL), F32)],
        compiler_params=_params(("parallel", "arbitrary")),
    )(x, g, w_in, w_in, w_out, g_next)
    return res if emit_hn else (res[0], None)


def _matmul_kernel(a_ref, b_ref, o_ref):
    o_ref[...] = _dot(a_ref[...], b_ref[...])


def _matmul(a, b, tn):
    rows, k = a.shape
    n = b.shape[1]
    tm = _tile(rows, 1024)
    return pl.pallas_call(
        _matmul_kernel,
        grid=(rows // tm, n // tn),
        in_specs=[pl.BlockSpec((tm, k), lambda i, j: (i, 0)), pl.BlockSpec((k, tn), lambda i, j: (0, j))],
        out_specs=pl.BlockSpec((tm, tn), lambda i, j: (i, j)),
        out_shape=jax.ShapeDtypeStruct((rows, n), F32),
        compiler_params=_params(("parallel", "arbitrary")),
    )(a, b)


def _qk_prep_kernel(q_ref, k_ref, f_ref, qg_ref, kg_ref, bf_ref, ones_ref, qn_ref, kn_ref, lf_ref, lfp_ref):
    ones_bd = ones_ref[...]

    def headnorm(x, g):
        ms = _segsum(x * x, ones_bd) * (1.0 / HEAD_DIM)
        return x * lax.rsqrt(ms + NORM_EPS) * g

    qn_ref[...] = (headnorm(q_ref[...], qg_ref[...]) * (HEAD_DIM ** -0.5)).astype(BF16)
    kn_ref[...] = headnorm(k_ref[...], kg_ref[...])
    lf = -_softplus(-(f_ref[...] + bf_ref[...]))
    lf_ref[...] = lf
    lane = lax.broadcasted_iota(jnp.int32, lf.shape, 1)
    for hp in range(N_HEADS // 2):
        a = lf[:, 2 * hp:2 * hp + 1]
        b = lf[:, 2 * hp + 1:2 * hp + 2]
        lfp_ref[:, hp * LANE:(hp + 1) * LANE] = jnp.where(lane == 0, a, jnp.where(lane == 1, b, 0.0))


def _qk_prep(qkvu, qg, kg, bf, ones_bd):
    rows = qkvu.shape[0]
    tm = _tile(rows, 1024)
    small = lambda w: pl.BlockSpec((1, w), lambda i: (0, 0))
    return pl.pallas_call(
        _qk_prep_kernel,
        grid=(rows // tm,),
        in_specs=[
            pl.BlockSpec((tm, MIX), lambda i: (i, 0)),
            pl.BlockSpec((tm, MIX), lambda i: (i, 1)),
            pl.BlockSpec((tm, LANE), lambda i: (i, 4 * MIX // LANE)),
            small(MIX), small(MIX), small(LANE),
            pl.BlockSpec((MIX, MIX), lambda i: (0, 0)),
        ],
        out_specs=[
            pl.BlockSpec((tm, MIX), lambda i: (i, 0)),
            pl.BlockSpec((tm, MIX), lambda i: (i, 0)),
            pl.BlockSpec((tm, LANE), lambda i: (i, 0)),
            pl.BlockSpec((tm, MIX), lambda i: (i, 0)),
        ],
        out_shape=[
            jax.ShapeDtypeStruct((rows, MIX), BF16),
            jax.ShapeDtypeStruct((rows, MIX), F32),
            jax.ShapeDtypeStruct((rows, LANE), F32),
            jax.ShapeDtypeStruct((rows, MIX), F32),
        ],
        compiler_params=_params(("parallel",)),
    )(qkvu, qkvu, qkvu, qg, kg, bf, ones_bd)


def _fox_attn_kernel(q_ref, k_ref, v_ref, lf_ref, o_ref, c_s):
    tp = q_ref.shape[0]
    cb = LANE
    row = lax.broadcasted_iota(jnp.int32, (cb, cb), 0)
    col = lax.broadcasted_iota(jnp.int32, (cb, cb), 1)
    tri = (row >= col).astype(F32)
    carry = jnp.zeros((1, LANE), F32)
    for blk in range(tp // cb):
        cblk = _dot_exact_lhs(tri, lf_ref[blk * cb:(blk + 1) * cb, :]) + carry
        c_s[blk * cb:(blk + 1) * cb, :] = cblk
        carry = cblk[cb - 1:cb, :]
    c = c_s[...]
    c_t = c.T
    q = q_ref[...]
    k = k_ref[...].astype(BF16)
    v = v_ref[...].astype(BF16)
    lane = lax.broadcasted_iota(jnp.int32, (1, LANE), 1)
    for r0 in range(0, tp, ATT_Q_TILE):
        r1 = min(r0 + ATT_Q_TILE, tp)
        tq = r1 - r0
        qi = lax.broadcasted_iota(jnp.int32, (tq, r1), 0) + r0
        ki = lax.broadcasted_iota(jnp.int32, (tq, r1), 1)
        causal = ki <= qi
        outs = []
        for e in range(2):
            head = (lane // HEAD_DIM) == e
            qe = jnp.where(head, q[r0:r1, :], jnp.zeros((), BF16))
            s = _dot(qe, k[:r1, :], NT)
            s = s + c[r0:r1, e:e + 1] - c_t[e:e + 1, :r1]
            s = jnp.where(causal, s, -1e30)
            m = jnp.max(s, axis=-1, keepdims=True)
            p = jnp.exp(s - m)
            l = jnp.sum(p, axis=-1, keepdims=True)
            outs.append(_dot(p.astype(BF16), v[:r1, :]) / l)
        o_ref[r0:r1, :] = jnp.where((lane // HEAD_DIM) == 0, outs[0], outs[1]).astype(o_ref.dtype)


def _fox_attention(qn, kn, qkvu, lfp, bsz, tp):
    hp = N_HEADS // 2
    out = pl.pallas_call(
        _fox_attn_kernel,
        grid=(bsz, hp),
        in_specs=[
            pl.BlockSpec((None, tp, LANE), lambda b, h: (b, 0, h)),
            pl.BlockSpec((None, tp, LANE), lambda b, h: (b, 0, h)),
            pl.BlockSpec((None, tp, LANE), lambda b, h: (b, 0, 2 * MIX // LANE + h)),
            pl.BlockSpec((None, tp, LANE), lambda b, h: (b, 0, h)),
        ],
        out_specs=pl.BlockSpec((None, tp, LANE), lambda b, h: (b, 0, h)),
        out_shape=jax.ShapeDtypeStruct((bsz, tp, MIX), BF16),
        scratch_shapes=[pltpu.VMEM((tp, LANE), F32)],
        compiler_params=_params(("parallel", "parallel")),
    )(qn.reshape(bsz, tp, MIX), kn.reshape(bsz, tp, MIX), qkvu.reshape(bsz, tp, QKVU_COLS),
      lfp.reshape(bsz, tp, MIX))
    return out.reshape(bsz * tp, MIX)


def _fox_decode_kernel(pt_ref, q_ref, kn_ref, vn_ref, lfn_ref, *rest, pages_per_step):
    del pt_ref
    g = pages_per_step
    k_refs, v_refs, lf_refs = rest[:g], rest[g:2 * g], rest[2 * g:3 * g]
    o_ref, m_s, l_s, acc_s, carry_s = rest[3 * g:]
    p = pl.program_id(1)
    hrow = lax.broadcasted_iota(jnp.int32, (N_HEADS, MIX), 0)
    hlane = lax.broadcasted_iota(jnp.int32, (N_HEADS, MIX), 1) // HEAD_DIM
    headmask = hrow == hlane
    qb = jnp.where(headmask, q_ref[...], 0.0)

    @pl.when(p == 0)
    def _():
        m_s[...] = jnp.sum(qb * kn_ref[...], axis=1, keepdims=True)
        l_s[...] = jnp.ones_like(l_s)
        acc_s[...] = jnp.where(headmask, vn_ref[...], 0.0)
        carry_s[...] = lfn_ref[...]

    srow = lax.broadcasted_iota(jnp.int32, (PAGE, PAGE), 0)
    scol = lax.broadcasted_iota(jnp.int32, (PAGE, PAGE), 1)
    later = (srow > scol).astype(F32)
    qbb = qb.astype(BF16)
    carry = carry_s[...]
    logits = []
    for i in range(g):
        lf = lf_refs[i][...]
        kt = k_refs[i][...].reshape(MIX, PAGE).astype(BF16)
        logits.append(_dot(qbb, kt) + (carry + _dot_exact_rhs(lf, later)))
        carry = carry + jnp.sum(lf, axis=1, keepdims=True)
    carry_s[...] = carry
    m_old = m_s[...]
    m_new = m_old
    for s in logits:
        m_new = jnp.maximum(m_new, jnp.max(s, axis=1, keepdims=True))
    alpha = jnp.exp(m_old - m_new)
    l = l_s[...] * alpha
    acc = acc_s[...] * alpha
    for i in range(g):
        pr = jnp.exp(logits[i] - m_new)
        l = l + jnp.sum(pr, axis=1, keepdims=True)
        acc = acc + _dot(pr.astype(BF16), v_refs[i][...].reshape(MIX, PAGE).astype(BF16), NT)
    m_s[...], l_s[...], acc_s[...] = m_new, l, acc

    @pl.when(p == pl.num_programs(1) - 1)
    def _():
        o_ref[...] = jnp.sum(jnp.where(headmask, acc / l, 0.0), axis=0, keepdims=True)


def _fox_decode(layer, q, kn, vn, lfn, cache_kt, cache_vt, cache_lft, page_table):
    bsz, n_pages = page_table.shape
    g = DECODE_PAGES_PER_STEP
    assert n_pages % g == 0

    def page_spec(block, i):
        return pl.BlockSpec((None, None) + block,
                            lambda b, p, pt: (layer, pt[b, n_pages - 1 - (p * g + i)]) + (0,) * len(block))

    tok = lambda r, w: pl.BlockSpec((None, r, w), lambda b, p, pt: (b, 0, 0))
    in_specs = [tok(1, MIX), tok(1, MIX), tok(1, MIX), tok(N_HEADS, 1)]
    in_specs += [page_spec((N_HEADS, HEAD_DIM, PAGE), i) for i in range(g)]
    in_specs += [page_spec((N_HEADS, HEAD_DIM, PAGE), i) for i in range(g)]
    in_specs += [page_spec((N_HEADS, PAGE), i) for i in range(g)]
    grid_spec = pltpu.PrefetchScalarGridSpec(
        num_scalar_prefetch=1,
        grid=(bsz, n_pages // g),
        in_specs=in_specs,
        out_specs=tok(1, MIX),
        scratch_shapes=[pltpu.VMEM((N_HEADS, 1), F32), pltpu.VMEM((N_HEADS, 1), F32),
                        pltpu.VMEM((N_HEADS, MIX), F32), pltpu.VMEM((N_HEADS, 1), F32)],
    )
    return pl.pallas_call(
        functools.partial(_fox_decode_kernel, pages_per_step=g),
        grid_spec=grid_spec,
        out_shape=jax.ShapeDtypeStruct((bsz, 1, MIX), F32),
        compiler_params=_params(("parallel", "arbitrary")),
    )(page_table, q, kn, vn, lfn, *([cache_kt] * g), *([cache_vt] * g), *([cache_lft] * g))


def _rwkv_prep_kernel(rw_ref, prev_ref, sh0_ref, mu_ref, w0_ref, w2_ref, a0_ref, a2_ref, g2_ref,
                      kk_ref, ka_ref, rk_ref, ones_ref,
                      r_out, lw_out, k_out, v_out, kk_out, b_out, g_out, bonus_out):
    i = pl.program_id(1)
    rw = rw_ref[...]
    tm = rw.shape[0]
    first = jnp.where(i == 0, sh0_ref[...], prev_ref[SUBLANE - 1:SUBLANE, :])
    trow = lax.broadcasted_iota(jnp.int32, (tm, 1), 0)
    prev = jnp.where(trow == 0, first, pltpu.roll(rw, 1, 0))
    z = rw + (prev - rw) * mu_ref[...]
    r = z[:, 0:MIX]
    k = z[:, MIX:2 * MIX]
    v = z[:, 2 * MIX:3 * MIX]
    xwa = z[:, 3 * MIX:3 * MIX + 2 * LORA_WA]
    xg = z[:, 3 * MIX + 2 * LORA_WA:]
    ones_bd = ones_ref[...]
    w = -_softplus(-(w0_ref[...] + _dot(jnp.tanh(xwa).astype(BF16), w2_ref[...]))) - 0.5
    lw_out[...] = -jnp.exp(w)
    a = jax.nn.sigmoid(a0_ref[...] + _dot(xwa.astype(BF16), a2_ref[...]))
    g_out[...] = _dot(jax.nn.sigmoid(xg).astype(BF16), g2_ref[...])
    kk = k * kk_ref[...]
    kk = kk / jnp.maximum(jnp.sqrt(_segsum(kk * kk, ones_bd)), 1e-12)
    k2 = k * (1.0 + (a - 1.0) * ka_ref[...])
    r_out[...] = r.astype(BF16)
    k_out[...] = k2.astype(BF16)
    v_out[...] = v.astype(BF16)
    kk_out[...] = kk.astype(BF16)
    b_out[...] = (kk * a).astype(BF16)
    bonus_out[...] = _segsum(r * k2 * rk_ref[...], ones_bd) * v


def _rwkv_prep(rwp, shift0, prm, bsz, tp):
    tm = _tile(tp, 544)
    nb = tm // SUBLANE
    small = lambda a: pl.BlockSpec(a.shape, lambda b, i: (0, 0))
    consts = [prm["mu"], prm["w0"], prm["w2"], prm["a0"], prm["a2"], prm["g2"], prm["k_k"], prm["k_a"],
              prm["r_k"], prm["ones_bd"]]
    rw3 = rwp.reshape(bsz, tp, RW_PAD)
    outs = pl.pallas_call(
        _rwkv_prep_kernel,
        grid=(bsz, tp // tm),
        in_specs=[
            pl.BlockSpec((None, tm, RW_PAD), lambda b, i: (b, i, 0)),
            pl.BlockSpec((None, SUBLANE, RW_PAD), lambda b, i: (b, jnp.maximum(i * nb - 1, 0), 0)),
            pl.BlockSpec((None, 1, RW_PAD), lambda b, i: (b, 0, 0)),
        ] + [small(c) for c in consts],
        out_specs=[pl.BlockSpec((None, tm, MIX), lambda b, i: (b, i, 0))] * 8,
        out_shape=[jax.ShapeDtypeStruct((bsz, tp, MIX), dt) for dt in (BF16, F32, BF16, BF16, BF16, BF16, F32, F32)],
        compiler_params=_params(("parallel", "parallel")),
    )(rw3, rw3, shift0, *consts)
    return outs


def _rwkv_scan_kernel(r_ref, lw_ref, k_ref, v_ref, kk_ref, b_ref, s0_ref, o_ref, s_out, s_s, *, chunk, t_valid):
    c = pl.program_id(1)
    n = chunk

    @pl.when(c == 0)
    def _():
        s_s[...] = s0_ref[...]

    nseq = r_ref.shape[0]
    trow = lax.broadcasted_iota(jnp.int32, (n, 1), 0) + c * n
    valid = trow < t_valid
    row = lax.broadcasted_iota(jnp.int32, (n, n), 0)
    col = lax.broadcasted_iota(jnp.int32, (n, n), 1)
    incl = (row >= col).astype(F32)
    eye = (row == col).astype(F32)
    row2 = lax.broadcasted_iota(jnp.int32, (n, 2 * n), 0)
    col2 = lax.broadcasted_iota(jnp.int32, (n, 2 * n), 1)
    col2 = jnp.where(col2 >= n, col2 - n, col2)
    strict2 = row2 > col2
    incl2 = row2 >= col2

    a_m, b_m, bk, v_h, wl_h, s_all = [], [], [], [], [], []
    for i in range(nseq):
        lw = jnp.where(valid, lw_ref[i], 0.0)
        kk = jnp.where(valid, kk_ref[i].astype(F32), 0.0)
        bb = jnp.where(valid, b_ref[i].astype(F32), 0.0)
        kx = jnp.where(valid, k_ref[i].astype(F32), 0.0)
        vx = jnp.where(valid, v_ref[i].astype(F32), 0.0)
        rx = r_ref[i].astype(F32)
        ci = _dot_exact_lhs(incl, lw)
        kkd = kk * jnp.exp(ci - lw)
        rd = rx * jnp.exp(ci)
        einv = jnp.exp(-ci)
        bd = bb * einv
        kd = kx * einv
        cl = ci[n - 1:n, :]
        wl = jnp.exp(cl)
        dl = jnp.exp(cl - ci)
        bdw = bb * dl
        kdw = kx * dl
        for h in range(N_HEADS):
            sl = slice(h * HEAD_DIM, (h + 1) * HEAD_DIM)
            a_m.append(jnp.concatenate([kkd[:, sl], rd[:, sl]], axis=0))
            b_m.append(jnp.concatenate([bd[:, sl], kd[:, sl]], axis=0))
            bk.append(jnp.concatenate([bdw[:, sl], kdw[:, sl]], axis=0))
            v_h.append(vx[:, sl])
            wl_h.append(wl[:, sl])
            s_all.append(s_s[i, h])
    ch = range(nseq * N_HEADS)
    pm = [_dot1(a_m[j], b_m[j], NT) for j in ch]
    p_top = [jnp.where(strict2, pm[j][:n, :], 0.0) for j in ch]
    p_bot = [jnp.where(incl2, pm[j][n:, :], 0.0) for j in ch]
    gm = [_dot1(a_m[j], s_all[j], NT) for j in ch]
    x = [-p_top[j][:, :n] for j in ch]
    tinv = [eye + x[j] for j in ch]
    span = 2
    while span < n:
        x = [_dot1(x[j], x[j]) for j in ch]
        tinv = [tinv[j] + _dot1(tinv[j], x[j]) for j in ch]
        span *= 2
    rhs = [gm[j][:n, :] + _dot1(p_top[j][:, n:], v_h[j]) for j in ch]
    u = [-_dot1(tinv[j], rhs[j]) for j in ch]
    uv = [jnp.concatenate([u[j], v_h[j]], axis=0) for j in ch]
    outs = [gm[j][n:, :] + _dot1(p_bot[j], uv[j]) for j in ch]
    s_new = [s_all[j] * wl_h[j] + _dot1(uv[j], bk[j], TN) for j in ch]
    for i in range(nseq):
        for h in range(N_HEADS):
            s_s[i, h] = s_new[i * N_HEADS + h]
        o_ref[i] = jnp.concatenate(outs[i * N_HEADS:(i + 1) * N_HEADS], axis=1)

    @pl.when(c == pl.num_programs(1) - 1)
    def _():
        s_out[...] = s_s[...]


def _rwkv_scan(r, lw, k, v, kk, b, s0, bsz, tp, t_valid, chunk):
    ns = RWKV_SEQS_PER_STEP
    assert bsz % ns == 0
    blk = pl.BlockSpec((ns, chunk, MIX), lambda bi, c: (bi, c, 0))
    st = pl.BlockSpec((ns, N_HEADS, HEAD_DIM, HEAD_DIM), lambda bi, c: (bi, 0, 0, 0))
    o, s_out = pl.pallas_call(
        functools.partial(_rwkv_scan_kernel, chunk=chunk, t_valid=t_valid),
        grid=(bsz // ns, tp // chunk),
        in_specs=[blk] * 6 + [st],
        out_specs=[blk, st],
        out_shape=[jax.ShapeDtypeStruct((bsz, tp, MIX), F32),
                   jax.ShapeDtypeStruct((bsz, N_HEADS, HEAD_DIM, HEAD_DIM), F32)],
        scratch_shapes=[pltpu.VMEM((ns, N_HEADS, HEAD_DIM, HEAD_DIM), F32)],
        compiler_params=_params(("parallel", "arbitrary")),
    )(r, lw, k, v, kk, b, s0)
    return o.reshape(bsz * tp, MIX), s_out


def _s5_kernel(u_ref, x0_ref, a_ref, bm_ref, cm_ref, d_ref, gw_ref, gb_ref, o_ref, xo_ref, buf_s, st_s,
               *, t_valid):
    c = pl.program_id(0)
    bsz, tc, _ = u_ref.shape
    tcp = buf_s.shape[1] // bsz
    half = S5_LANES // 2
    hc = MIX // 2
    nblk = S5_LANES // LANE
    hblk = half // LANE

    @pl.when(c == 0)
    def _():
        st_s[...] = x0_ref[...]

    u2 = u_ref[...].reshape(bsz * tc, MIX)
    ub = u2.astype(BF16)
    for part in range(2):
        for j in range(2):
            col0 = part * S5_LANES + j * half
            res = _dot(ub[:, j * hc:(j + 1) * hc], bm_ref[j * hc:(j + 1) * hc, col0:col0 + half])
            for k in range(hblk):
                for b in range(bsz):
                    buf_s[col0 // LANE + k, b * tcp:b * tcp + tc, :] = res[b * tc:(b + 1) * tc,
                                                                           k * LANE:(k + 1) * LANE]

    gb = max(1, 2 * nblk // bsz)
    for g0 in range(0, nblk, gb):
        blks = list(range(g0, g0 + gb))
        lane = lambda k: slice(k * LANE, (k + 1) * LANE)
        ar = [jnp.broadcast_to(a_ref[:, lane(k)], (bsz, LANE)) for k in blks]
        ai = [jnp.broadcast_to(a_ref[:, lane(nblk + k)], (bsz, LANE)) for k in blks]

        def body(t, carry, blks=blks, ar=ar, ai=ai):
            rows = pl.ds(t, bsz, stride=tcp)
            out = []
            for n, k in enumerate(blks):
                xr, xi = carry[n]
                nxr = ar[n] * xr - ai[n] * xi + buf_s[k, rows, :]
                nxi = ar[n] * xi + ai[n] * xr + buf_s[nblk + k, rows, :]
                buf_s[k, rows, :] = nxr
                buf_s[nblk + k, rows, :] = nxi
                out.append((nxr, nxi))
            return tuple(out)

        init = tuple((st_s[:, lane(k)], st_s[:, lane(nblk + k)]) for k in blks)
        fin = lax.fori_loop(0, tc, body, init)
        for n, k in enumerate(blks):
            st_s[:, lane(k)] = fin[n][0]
            st_s[:, lane(nblk + k)] = fin[n][1]

    c_valid, t_in = divmod(t_valid - 1, tc)

    @pl.when(c == c_valid)
    def _():
        for k in range(2 * nblk):
            xo_ref[:, k * LANE:(k + 1) * LANE] = buf_s[k, pl.ds(t_in, bsz, stride=tcp), :]

    ys = []
    for j in range(2):
        acc = None
        for part in range(2):
            col0 = part * S5_LANES + j * half
            xs = jnp.concatenate(
                [jnp.concatenate([buf_s[col0 // LANE + k, b * tcp:b * tcp + tc, :] for b in range(bsz)], axis=0)
                 for k in range(hblk)], axis=1)
            d = _dot(xs.astype(BF16), cm_ref[col0:col0 + half, j * hc:(j + 1) * hc])
            acc = d if acc is None else acc + d
        ys.append(acc)
    y = jnp.concatenate(ys, axis=1) + d_ref[...] * u2
    y = 0.5 * y * (1.0 + jnp.tanh(0.7978845608028654 * (y + 0.044715 * (y * y * y))))
    gate = jax.nn.sigmoid(_dot(y.astype(BF16), gw_ref[...]) + gb_ref[...])
    o_ref[...] = (y * gate).reshape(bsz, tc, MIX).astype(o_ref.dtype)


def _s5(qkvu, x0, prm, bsz, tp, t_valid, tc):
    n2 = 2 * S5_LANES
    full = lambda a: pl.BlockSpec(a.shape, lambda c: (0,) * a.ndim)
    consts = [prm["s5_a"], prm["s5_bm"], prm["s5_cm"], prm["s5_d"], prm["s5_gw"], prm["s5_gb"]]
    o, xo = pl.pallas_call(
        functools.partial(_s5_kernel, t_valid=t_valid),
        grid=(tp // tc,),
        in_specs=[pl.BlockSpec((bsz, tc, MIX), lambda c: (0, c, 3)), full(x0)] + [full(a) for a in consts],
        out_specs=[pl.BlockSpec((bsz, tc, MIX), lambda c: (0, c, 0)), pl.BlockSpec((bsz, n2), lambda c: (0, 0))],
        out_shape=[jax.ShapeDtypeStruct((bsz, tp, MIX), BF16), jax.ShapeDtypeStruct((bsz, n2), F32)],
        scratch_shapes=[pltpu.VMEM((n2 // LANE, bsz * (tc + S5_SLAB_PAD), LANE), F32),
                        pltpu.VMEM((bsz, n2), F32)],
        compiler_params=_params(("arbitrary",)),
    )(qkvu.reshape(bsz, tp, QKVU_COLS), x0, *consts)
    return o.reshape(bsz * tp, MIX), xo


def _merge_kernel(x_ref, hn_ref, oa_ref, ob_ref, g_ref, bonus_ref, oc_ref, lw_ref, lb_ref, ones_ref,
                  wg_ref, wb_ref, wo_ref, o_ref):
    ones_bd = ones_ref[...]
    o = ob_ref[...]
    mu = _segsum(o, ones_bd) * (1.0 / HEAD_DIM)
    dlt = o - mu
    var = _segsum(dlt * dlt, ones_bd) * (1.0 / HEAD_DIM)
    ob = (dlt * lax.rsqrt(var + LNX_EPS) * lw_ref[...] + lb_ref[...] + bonus_ref[...]) * g_ref[...]
    hn = hn_ref[...]
    branches = (oa_ref[...].astype(BF16), ob.astype(BF16), oc_ref[...].astype(BF16))
    merged = None
    for i, br in enumerate(branches):
        gate = jax.nn.sigmoid(_dot(hn, wg_ref[:, i * D_MODEL:(i + 1) * D_MODEL]))
        term = gate * _dot(br, wb_ref[i * MIX:(i + 1) * MIX, :])
        merged = term if merged is None else merged + term
    o_ref[...] = x_ref[...] + _dot(merged.astype(BF16), wo_ref[...])


def _merge(x, hn, oa, ob, g, bonus, oc, prm):
    rows = x.shape[0]
    tm = _tile(rows, 512)
    row = lambda w: pl.BlockSpec((tm, w), lambda i: (i, 0))
    full = lambda a: pl.BlockSpec(a.shape, lambda i: (0, 0))
    consts = [prm["lnx_w"], prm["lnx_b"], prm["ones_bd"], prm["w_gates"], prm["w_branch"], prm["w_out"]]
    return pl.pallas_call(
        _merge_kernel,
        grid=(rows // tm,),
        in_specs=[row(D_MODEL), row(D_MODEL), row(MIX), row(MIX), row(MIX), row(MIX), row(MIX)]
        + [full(c) for c in consts],
        out_specs=row(D_MODEL),
        out_shape=jax.ShapeDtypeStruct((rows, D_MODEL), F32),
        compiler_params=_params(("parallel",)),
    )(x, hn, oa, ob, g, bonus, oc, *consts)


def _layer_params(l, p):
    bf = lambda a: a.astype(BF16)
    row = lambda a, w=None: (a if w is None else jnp.pad(a, (0, w - a.shape[0]))).reshape(1, -1).astype(F32)
    w_in = p["w_in"][l]
    o_f = 3 * MIX
    o_rw = o_f + N_HEADS
    o_u = o_rw + RW_COLS
    o_g = o_u + MIX
    w_qkvu = jnp.concatenate(
        [w_in[:, :o_f], w_in[:, o_u:o_g], jnp.pad(w_in[:, o_f:o_rw], ((0, 0), (0, LANE - N_HEADS)))], axis=1)
    w_rw = jnp.pad(w_in[:, o_rw:o_u], ((0, 0), (0, RW_PAD - RW_COLS)))
    w_gates = w_in[:, o_g:]

    ar = p["s5_a_re"][l].astype(F32)
    ai = p["s5_a_im"][l].astype(F32)
    dt = jnp.exp(p["s5_log_dt"][l].astype(F32))[:, None]
    mag = jnp.exp(dt * ar)
    abar_re = mag * jnp.cos(dt * ai)
    abar_im = mag * jnp.sin(dt * ai)
    den = ar * ar + ai * ai
    nr, ni = abar_re - 1.0, abar_im
    z_re = (nr * ar + ni * ai) / den
    z_im = (ni * ar - nr * ai) / den
    b_re = p["s5_b_re"][l].astype(F32)
    b_im = p["s5_b_im"][l].astype(F32)
    bb_re = z_re[..., None] * b_re - z_im[..., None] * b_im
    bb_im = z_re[..., None] * b_im + z_im[..., None] * b_re
    eye = jnp.eye(S5_GROUPS, dtype=F32)
    blk_in = lambda m: jnp.einsum("gpc,gh->gchp", m, eye).reshape(MIX, S5_LANES)
    blk_out = lambda m: jnp.einsum("gcp,gh->gphc", m, eye).reshape(S5_LANES, MIX)
    s5_bm = jnp.concatenate([blk_in(bb_re), blk_in(bb_im)], axis=1)
    s5_cm = jnp.concatenate([blk_out(p["s5_c_re"][l].astype(F32)), -blk_out(p["s5_c_im"][l].astype(F32))], axis=0)

    hd = lax.broadcasted_iota(jnp.int32, (MIX, MIX), 0) // HEAD_DIM
    hd_t = lax.broadcasted_iota(jnp.int32, (MIX, MIX), 1) // HEAD_DIM
    return dict(
        ffn1_norm=row(p["ffn1_norm"][l]), ffn1_w_in=bf(p["ffn1_w_in"][l]), ffn1_w_out=bf(p["ffn1_w_out"][l]),
        ffn2_norm=row(p["ffn2_norm"][l]), ffn2_w_in=bf(p["ffn2_w_in"][l]), ffn2_w_out=bf(p["ffn2_w_out"][l]),
        mix_norm=row(p["mix_norm"][l]),
        w_qkvu=bf(w_qkvu), w_rw=bf(w_rw), w_gates=bf(w_gates),
        q_gain=row(jnp.tile(p["fox_q_gain"][l], N_HEADS)), k_gain=row(jnp.tile(p["fox_k_gain"][l], N_HEADS)),
        b_f=row(p["fox_b_f"][l], LANE),
        ones_bd=(hd == hd_t).astype(BF16),
        mu=row(p["rwkv_mu"][l], RW_PAD), w0=row(p["rwkv_w0"][l]), a0=row(p["rwkv_a0"][l]),
        w2=bf(jnp.pad(p["rwkv_w2"][l], ((0, LORA_WA), (0, 0)))),
        a2=bf(jnp.pad(p["rwkv_a2"][l], ((LORA_WA, 0), (0, 0)))),
        g2=bf(jnp.pad(p["rwkv_g2"][l], ((0, RW_PAD - RW_COLS), (0, 0)))),
        k_k=row(p["rwkv_k_k"][l]), k_a=row(p["rwkv_k_a"][l]), r_k=row(p["rwkv_r_k"][l].reshape(-1)),
        lnx_w=row(p["rwkv_lnx_w"][l]), lnx_b=row(p["rwkv_lnx_b"][l]),
        s5_a=jnp.concatenate([abar_re.reshape(1, -1), abar_im.reshape(1, -1)], axis=1),
        s5_bm=bf(s5_bm), s5_cm=bf(s5_cm), s5_d=row(p["s5_d"][l].reshape(-1)),
        s5_gw=bf(p["s5_glu_w"][l]), s5_gb=row(p["s5_glu_b"][l]),
        w_branch=bf(p["w_branch"][l]), w_out=bf(p["w_out"][l]),
    )


def _mixer_inputs(x, prm):
    x, hn = _ffn(x, prm["ffn1_norm"], prm["ffn1_w_in"], prm["ffn1_w_out"], prm["mix_norm"], True)
    qkvu = _matmul(hn, prm["w_qkvu"], QKVU_COLS)
    rwp = _matmul(hn, prm["w_rw"], RW_PAD)
    qn, kn, lf, lfp = _qk_prep(qkvu, prm["q_gain"], prm["k_gain"], prm["b_f"], prm["ones_bd"])
    return x, qkvu, rwp, hn, qn, kn, lf, lfp


def _layer_tail(x, oa, qkvu, rwp, hn, prm, shift0, s0, x0, bsz, tp, t_valid, chunk, tc):
    r, lw, k2, v, kk, b, g, bonus = _rwkv_prep(rwp, shift0, prm, bsz, tp)
    ob, s_out = _rwkv_scan(r, lw, k2, v, kk, b, s0, bsz, tp, t_valid, chunk)
    oc, xo = _s5(qkvu, x0, prm, bsz, tp, t_valid, tc)
    flat = lambda a: a.reshape(bsz * tp, MIX)
    x = _merge(x, hn, oa, ob, flat(g), flat(bonus), oc, prm)
    x, _ = _ffn(x, prm["ffn2_norm"], prm["ffn2_w_in"], prm["ffn2_w_out"], prm["ffn2_norm"], False)
    return x, s_out, xo


def kernel(x_prompt, x_sample, cache_k, cache_v, cache_logf, page_table, state_rwkv, state_shift, state_s5_re, state_s5_im, meta_tokens, ffn1_norm, ffn1_w_in, ffn1_w_out, mix_norm, w_in, fox_b_f, fox_q_gain, fox_k_gain, rwkv_mu, rwkv_w0, rwkv_w2, rwkv_a0, rwkv_a2, rwkv_g2, rwkv_k_k, rwkv_k_a, rwkv_r_k, rwkv_lnx_w, rwkv_lnx_b, s5_a_re, s5_a_im, s5_log_dt, s5_b_re, s5_b_im, s5_c_re, s5_c_im, s5_d, s5_glu_w, s5_glu_b, w_branch, w_out, ffn2_norm, ffn2_w_in, ffn2_w_out):
    p = dict(ffn1_norm=ffn1_norm, ffn1_w_in=ffn1_w_in, ffn1_w_out=ffn1_w_out, mix_norm=mix_norm,
             w_in=w_in, fox_b_f=fox_b_f, fox_q_gain=fox_q_gain, fox_k_gain=fox_k_gain,
             rwkv_mu=rwkv_mu, rwkv_w0=rwkv_w0, rwkv_w2=rwkv_w2, rwkv_a0=rwkv_a0, rwkv_a2=rwkv_a2,
             rwkv_g2=rwkv_g2, rwkv_k_k=rwkv_k_k, rwkv_k_a=rwkv_k_a, rwkv_r_k=rwkv_r_k,
             rwkv_lnx_w=rwkv_lnx_w, rwkv_lnx_b=rwkv_lnx_b, s5_a_re=s5_a_re, s5_a_im=s5_a_im,
             s5_log_dt=s5_log_dt, s5_b_re=s5_b_re, s5_b_im=s5_b_im, s5_c_re=s5_c_re, s5_c_im=s5_c_im,
             s5_d=s5_d, s5_glu_w=s5_glu_w, s5_glu_b=s5_glu_b, w_branch=w_branch, w_out=w_out,
             ffn2_norm=ffn2_norm, ffn2_w_in=ffn2_w_in, ffn2_w_out=ffn2_w_out)
    depth = w_in.shape[0]
    prms = [_layer_params(l, p) for l in range(depth)]

    bp, seq, _ = x_prompt.shape
    t_p = seq + N_META
    tp = -(-t_p // SEQ_ALIGN) * SEQ_ALIGN
    meta = jnp.broadcast_to(meta_tokens.astype(F32)[None], (bp, N_META, D_MODEL))
    xp = jnp.concatenate([meta, x_prompt], axis=1)
    xp = jnp.pad(xp, ((0, 0), (0, tp - t_p), (0, 0))).reshape(bp * tp, D_MODEL)
    z_shift = jnp.zeros((bp, 1, RW_PAD), F32)
    z_s = jnp.zeros((bp, N_HEADS, HEAD_DIM, HEAD_DIM), F32)
    z_x = jnp.zeros((bp, 2 * S5_LANES), F32)
    seqs = lambda a, w: a.reshape(bp, tp, w)
    pk, pv, plf, prw, psh, pre, pim = [], [], [], [], [], [], []
    for l in range(depth):
        prm = prms[l]
        xp, qkvu, rwp, hn, qn, kn, lf, lfp = _mixer_inputs(xp, prm)
        oa = _fox_attention(qn, kn, qkvu, lfp, bp, tp)
        xp, s_out, xo = _layer_tail(xp, oa, qkvu, rwp, hn, prm, z_shift, z_s, z_x, bp, tp, t_p,
                                    RWKV_CHUNK, S5_CHUNK)
        pk.append(seqs(kn, MIX)[:, :t_p].reshape(bp, t_p, N_HEADS, HEAD_DIM))
        pv.append(seqs(qkvu, QKVU_COLS)[:, :t_p, 2 * MIX:3 * MIX].reshape(bp, t_p, N_HEADS, HEAD_DIM))
        plf.append(seqs(lf, LANE)[:, :t_p, :N_HEADS])
        prw.append(s_out)
        psh.append(seqs(rwp, RW_PAD)[:, t_p - 1, :RW_COLS])
        pre.append(xo[:, :S5_LANES].reshape(bp, S5_GROUPS, S5_STATE))
        pim.append(xo[:, S5_LANES:].reshape(bp, S5_GROUPS, S5_STATE))
    y_prompt = seqs(xp, D_MODEL)[:, N_META:t_p]

    bs = x_sample.shape[0]
    ts = SUBLANE
    xs = jnp.pad(x_sample, ((0, 0), (0, ts - 1), (0, 0))).reshape(bs * ts, D_MODEL)
    ck =cache_k.transpose(0, 1, 3, 4, 2)
    cv = cache_v.transpose(0, 1, 3, 4, 2)
    clf = cache_logf.transpose(0, 1, 3, 2)
    tok0 = lambda a, w: a.reshape(bs, ts, w)[:, 0]
    sk, sv, slf, srw, ssh, sre, sim = [], [], [], [], [], [], []
    for l in range(depth):
        prm = prms[l]
        xs, qkvu, rwp, hn, qn, kn, lf, lfp = _mixer_inputs(xs, prm)
        q_new = tok0(qn, MIX).astype(F32)
        k_new = tok0(kn, MIX)
        v_new = tok0(qkvu, QKVU_COLS)[:, 2 * MIX:3 * MIX]
        lf_new = tok0(lf, LANE)[:, :N_HEADS]
        oa = _fox_decode(l, q_new[:, None], k_new[:, None], v_new[:, None], lf_new[:, :, None], ck, cv, clf,
                         page_table)
        oa = jnp.pad(oa, ((0, 0), (0, ts - 1), (0, 0))).reshape(bs * ts, MIX)
        shift0 = jnp.pad(state_shift[l].astype(F32), ((0, 0), (0, RW_PAD - RW_COLS)))[:, None]
        x0 = jnp.concatenate([state_s5_re[l].reshape(bs, S5_LANES), state_s5_im[l].reshape(bs, S5_LANES)],
                             axis=1).astype(F32)
        xs, s_out, xo = _layer_tail(xs, oa, qkvu, rwp, hn, prm, shift0, state_rwkv[l].astype(F32), x0,
                                    bs, ts, 1, ts, ts)
        sk.append(k_new.reshape(bs, 1, N_HEADS, HEAD_DIM))
        sv.append(v_new.reshape(bs, 1, N_HEADS, HEAD_DIM))
        slf.append(lf_new.reshape(bs, 1, N_HEADS))
        srw.append(s_out)
        ssh.append(tok0(rwp, RW_PAD)[:, :RW_COLS])
        sre.append(xo[:, :S5_LANES].reshape(bs, S5_GROUPS, S5_STATE))
        sim.append(xo[:, S5_LANES:].reshape(bs, S5_GROUPS, S5_STATE))
    y_sample = tok0(xs, D_MODEL).reshape(bs, 1, D_MODEL)

    st = lambda xs_: jnp.stack(xs_, axis=0)
    return (y_prompt, y_sample, st(pk), st(pv), st(plf), st(prw), st(psh), st(pre), st(pim),
            st(sk), st(sv), st(slf), st(srw), st(ssh), st(sre), st(sim))
```

```python
import functools

import jax
import jax.numpy as jnp
from jax import lax
from jax.experimental import pallas as pl
from jax.experimental.pallas import tpu as pltpu

F32 = jnp.float32
BF16 = jnp.bfloat16

D_MODEL = 1024
N_META = 16
HEAD_DIM = 64
MIX = D_MODEL // 2
N_HEADS = MIX // HEAD_DIM
D_FF = 2816
RW_COLS = 1824
RW_PAD = 1920
LORA_WA = 64
LORA_G = 160
S5_GROUPS = 32
S5_GROUP = 16
S5_STATE = 64
S5_LANES = S5_GROUPS * S5_STATE
QKVU_COLS = 4 * MIX + 128
PAGE = 128
NORM_EPS = 1e-6
LNX_EPS = 64e-5

LANE = 128
SEQ_ALIGN = 128
FF_TILE = 1408
FF_ROWS = 512
ATT_Q_TILE = 512
RWKV_CHUNK = 64
RWKV_SEQS_PER_STEP = 2
S5_CHUNK = 64
S5_SLAB_PAD = 8
SUBLANE = 8
DECODE_PAGES_PER_STEP = 32
VMEM_LIMIT = 56 * 1024 * 1024

NN = (((1,), (0,)), ((), ()))
NT = (((1,), (1,)), ((), ()))
TN = (((0,), (0,)), ((), ()))


def _params(sem):
    return pltpu.CompilerParams(dimension_semantics=sem, vmem_limit_bytes=VMEM_LIMIT)


def _tile(n, target):
    best = None
    for t in range(8, min(n, target) + 1, 8):
        if n % t == 0:
            best = t
    return best or n


def _dot(a, b, dn=NN):
    return lax.dot_general(a, b, dn, preferred_element_type=F32)


def _split2(x):
    hi = x.astype(BF16)
    return hi, (x - hi.astype(F32)).astype(BF16)


def _dot1(a, b, dn=NN):
    return _dot(a.astype(BF16), b.astype(BF16), dn)


def _dot_exact_lhs(a01, b):
    a = a01.astype(BF16)
    b1 = b.astype(BF16)
    r1 = b - b1.astype(F32)
    b2 = r1.astype(BF16)
    b3 = (r1 - b2.astype(F32)).astype(BF16)
    return _dot(a, b1) + _dot(a, b2) + _dot(a, b3)


def _dot_exact_rhs(a, b01):
    b = b01.astype(BF16)
    a1 = a.astype(BF16)
    r1 = a - a1.astype(F32)
    a2 = r1.astype(BF16)
    a3 = (r1 - a2.astype(F32)).astype(BF16)
    return _dot(a1, b) + _dot(a2, b) + _dot(a3, b)


def _segsum(x, ones_bd):
    hi, lo = _split2(x)
    return _dot(hi, ones_bd) + _dot(lo, ones_bd)


def _softplus(x):
    return jnp.maximum(x, 0.0) + jnp.log1p(jnp.exp(-jnp.abs(x)))


def _rms(x, g):
    ms = jnp.mean(x * x, axis=-1, keepdims=True)
    return x * lax.rsqrt(ms + NORM_EPS) * g


def _ffn_kernel(x_ref, g_ref, wg_ref, wu_ref, wo_ref, gn_ref, *rest, emit_hn):
    if emit_hn:
        o_ref, hn_ref, xn_s, acc_s = rest
    else:
        o_ref, xn_s, acc_s = rest
    j = pl.program_id(1)

    @pl.when(j == 0)
    def _():
        xn_s[...] = _rms(x_ref[...], g_ref[...]).astype(BF16)
        acc_s[...] = jnp.zeros_like(acc_s)

    xn = xn_s[...]
    gate = _dot(xn, wg_ref[...])
    up = _dot(xn, wu_ref[...])
    act = (gate * jax.nn.sigmoid(gate) * up).astype(BF16)
    acc_s[...] += _dot(act, wo_ref[...])

    @pl.when(j == pl.num_programs(1) - 1)
    def _():
        y = x_ref[...] + 0.5 * acc_s[...]
        o_ref[...] = y
        if emit_hn:
            hn_ref[...] = _rms(y, gn_ref[...]).astype(BF16)


def _ffn(x, g, w_in, w_out, g_next, emit_hn):
    rows = x.shape[0]
    tm = _tile(rows, FF_ROWS)
    nf = D_FF // FF_TILE
    out_shape = [jax.ShapeDtypeStruct((rows, D_MODEL), F32)]
    out_specs = [pl.BlockSpec((tm, D_MODEL), lambda i, j: (i, 0))]
    if emit_hn:
        out_shape.append(jax.ShapeDtypeStruct((rows, D_MODEL), BF16))
        out_specs.append(pl.BlockSpec((tm, D_MODEL), lambda i, j: (i, 0)))
    res = pl.pallas_call(
        functools.partial(_ffn_kernel, emit_hn=emit_hn),
        grid=(rows // tm, nf),
        in_specs=[
            pl.BlockSpec((tm, D_MODEL), lambda i, j: (i, 0)),
            pl.BlockSpec((1, D_MODEL), lambda i, j: (0, 0)),
            pl.BlockSpec((D_MODEL, FF_TILE), lambda i, j: (0, j)),
            pl.BlockSpec((D_MODEL, FF_TILE), lambda i, j: (0, j + nf)),
            pl.BlockSpec((FF_TILE, D_MODEL), lambda i, j: (j, 0)),
            pl.BlockSpec((1, D_MODEL), lambda i, j: (0, 0)),
        ],
        out_specs=out_specs,
        out_shape=out_shape,
        scratch_shapes=[pltpu.VMEM((tm, D_MODEL), BF16), pltpu.VMEM((tm, D_MODEL), F32)],
        compiler_params=_params(("parallel", "arbitrary")),
    )(x, g, w_in, w_in, w_out, g_next)
    return res if emit_hn else (res[0], None)


def _matmul_kernel(a_ref, b_ref, o_ref):
    o_ref[...] = _dot(a_ref[...], b_ref[...])


def _matmul(a, b, tn):
    rows, k = a.shape
    n = b.shape[1]
    tm = _tile(rows, 1024)
    return pl.pallas_call(
        _matmul_kernel,
        grid=(rows // tm, n // tn),
        in_specs=[pl.BlockSpec((tm, k), lambda i, j: (i, 0)), pl.BlockSpec((k, tn), lambda i, j: (0, j))],
        out_specs=pl.BlockSpec((tm, tn), lambda i, j: (i, j)),
        out_shape=jax.ShapeDtypeStruct((rows, n), F32),
        compiler_params=_params(("parallel", "arbitrary")),
    )(a, b)


def _qk_prep_kernel(q_ref, k_ref, v_ref, f_ref, qg_ref, kg_ref, bf_ref, ones_ref, qn_ref, kn_ref, lf_ref, lfp_ref,
                    *kv_out):
    ones_bd = ones_ref[...]

    def headnorm(x, g):
        ms = _segsum(x * x, ones_bd) * (1.0 / HEAD_DIM)
        return x * lax.rsqrt(ms + NORM_EPS) * g

    qn_ref[...] = (headnorm(q_ref[...], qg_ref[...]) * (HEAD_DIM ** -0.5)).astype(BF16)
    kn = headnorm(k_ref[...], kg_ref[...])
    kn_ref[...] = kn
    if kv_out:
        kv_out[0][...] = kn
        kv_out[1][...] = v_ref[...]
    lf = -_softplus(-(f_ref[...] + bf_ref[...]))
    lf_ref[...] = lf
    lane = lax.broadcasted_iota(jnp.int32, lf.shape, 1)
    for hp in range(N_HEADS // 2):
        a = lf[:, 2 * hp:2 * hp + 1]
        b = lf[:, 2 * hp + 1:2 * hp + 2]
        lfp_ref[:, hp * LANE:(hp + 1) * LANE] = jnp.where(lane == 0, a, jnp.where(lane == 1, b, 0.0))


def _qk_prep(qkvu, qg, kg, bf, ones_bd, bsz, tp, t_valid):
    tm = _tile(tp, 544)
    emit_kv = t_valid % SUBLANE == 0
    small = lambda w: pl.BlockSpec((1, w), lambda b, i: (0, 0))
    blk = lambda w, j=0: pl.BlockSpec((None, tm, w), lambda b, i: (b, i, j))
    out_specs = [blk(MIX), blk(MIX), blk(LANE), blk(MIX)]
    out_shape = [jax.ShapeDtypeStruct((bsz, tp, MIX), BF16), jax.ShapeDtypeStruct((bsz, tp, MIX), F32),
                 jax.ShapeDtypeStruct((bsz, tp, LANE), F32), jax.ShapeDtypeStruct((bsz, tp, MIX), F32)]
    if emit_kv:
        out_specs += [blk(MIX), blk(MIX)]
        out_shape += [jax.ShapeDtypeStruct((bsz, t_valid, MIX), F32)] * 2
    q3 = qkvu.reshape(bsz, tp, QKVU_COLS)
    res = pl.pallas_call(
        _qk_prep_kernel,
        grid=(bsz, tp // tm),
        in_specs=[blk(MIX, 0), blk(MIX, 1), blk(MIX, 2), blk(LANE, 4 * MIX // LANE),
                  small(MIX), small(MIX), small(LANE), pl.BlockSpec((MIX, MIX), lambda b, i: (0, 0))],
        out_specs=out_specs,
        out_shape=out_shape,
        compiler_params=_params(("parallel", "parallel")),
    )(q3, q3, q3, q3, qg, kg, bf, ones_bd)
    qn, kn, lf, lfp = (a.reshape(bsz * tp, a.shape[-1]) for a in res[:4])
    return qn, kn, lf, lfp, (res[4], res[5]) if emit_kv else None


def _fox_attn_kernel(q_ref, k_ref, v_ref, lf_ref, o_ref, c_s):
    tp = q_ref.shape[0]
    cb = LANE
    row = lax.broadcasted_iota(jnp.int32, (cb, cb), 0)
    col = lax.broadcasted_iota(jnp.int32, (cb, cb), 1)
    tri = (row >= col).astype(F32)
    carry = jnp.zeros((1, LANE), F32)
    for blk in range(tp // cb):
        cblk = _dot_exact_lhs(tri, lf_ref[blk * cb:(blk + 1) * cb, :]) + carry
        c_s[blk * cb:(blk + 1) * cb, :] = cblk
        carry = cblk[cb - 1:cb, :]
    c = c_s[...]
    c_t = c.T
    q = q_ref[...]
    k = k_ref[...].astype(BF16)
    v = v_ref[...].astype(BF16)
    lane = lax.broadcasted_iota(jnp.int32, (1, LANE), 1)
    for r0 in range(0, tp, ATT_Q_TILE):
        r1 = min(r0 + ATT_Q_TILE, tp)
        tq = r1 - r0
        qi = lax.broadcasted_iota(jnp.int32, (tq, r1), 0) + r0
        ki = lax.broadcasted_iota(jnp.int32, (tq, r1), 1)
        causal = ki <= qi
        outs = []
        for e in range(2):
            head = (lane // HEAD_DIM) == e
            qe = jnp.where(head, q[r0:r1, :], jnp.zeros((), BF16))
            s = _dot(qe, k[:r1, :], NT)
            s = s + c[r0:r1, e:e + 1] - c_t[e:e + 1, :r1]
            s = jnp.where(causal, s, -1e30)
            m = jnp.max(s, axis=-1, keepdims=True)
            p = jnp.exp(s - m)
            l = jnp.sum(p, axis=-1, keepdims=True)
            outs.append(_dot(p.astype(BF16), v[:r1, :]) / l)
        o_ref[r0:r1, :] = jnp.where((lane // HEAD_DIM) == 0, outs[0], outs[1]).astype(o_ref.dtype)


def _fox_attention(qn, kn, qkvu, lfp, bsz, tp):
    hp = N_HEADS // 2
    out = pl.pallas_call(
        _fox_attn_kernel,
        grid=(bsz, hp),
        in_specs=[
            pl.BlockSpec((None, tp, LANE), lambda b, h: (b, 0, h)),
            pl.BlockSpec((None, tp, LANE), lambda b, h: (b, 0, h)),
            pl.BlockSpec((None, tp, LANE), lambda b, h: (b, 0, 2 * MIX // LANE + h)),
            pl.BlockSpec((None, tp, LANE), lambda b, h: (b, 0, h)),
        ],
        out_specs=pl.BlockSpec((None, tp, LANE), lambda b, h: (b, 0, h)),
        out_shape=jax.ShapeDtypeStruct((bsz, tp, MIX), BF16),
        scratch_shapes=[pltpu.VMEM((tp, LANE), F32)],
        compiler_params=_params(("parallel", "parallel")),
    )(qn.reshape(bsz, tp, MIX), kn.reshape(bsz, tp, MIX), qkvu.reshape(bsz, tp, QKVU_COLS),
      lfp.reshape(bsz, tp, MIX))
    return out.reshape(bsz * tp, MIX)


def _fox_decode_kernel(pt_ref, q_ref, kn_ref, vn_ref, lfn_ref, *rest, pages_per_step):
    del pt_ref
    g = pages_per_step
    k_refs, v_refs, lf_refs = rest[:g], rest[g:2 * g], rest[2 * g:3 * g]
    o_ref, m_s, l_s, acc_s, carry_s = rest[3 * g:]
    p = pl.program_id(1)
    hrow = lax.broadcasted_iota(jnp.int32, (N_HEADS, MIX), 0)
    hlane = lax.broadcasted_iota(jnp.int32, (N_HEADS, MIX), 1) // HEAD_DIM
    headmask = hrow == hlane
    qb = jnp.where(headmask, q_ref[...], 0.0)

    @pl.when(p == 0)
    def _():
        m_s[...] = jnp.sum(qb * kn_ref[...], axis=1, keepdims=True)
        l_s[...] = jnp.ones_like(l_s)
        acc_s[...] = jnp.where(headmask, vn_ref[...], 0.0)
        carry_s[...] = lfn_ref[...]

    srow = lax.broadcasted_iota(jnp.int32, (PAGE, PAGE), 0)
    scol = lax.broadcasted_iota(jnp.int32, (PAGE, PAGE), 1)
    later = (srow > scol).astype(F32)
    qbb = qb.astype(BF16)
    carry = carry_s[...]
    logits = []
    for i in range(g):
        lf = lf_refs[i][...]
        kt = k_refs[i][...].reshape(MIX, PAGE).astype(BF16)
        logits.append(_dot(qbb, kt) + (carry + _dot_exact_rhs(lf, later)))
        carry = carry + jnp.sum(lf, axis=1, keepdims=True)
    carry_s[...] = carry
    m_old = m_s[...]
    m_new = m_old
    for s in logits:
        m_new = jnp.maximum(m_new, jnp.max(s, axis=1, keepdims=True))
    alpha = jnp.exp(m_old - m_new)
    l = l_s[...] * alpha
    acc = acc_s[...] * alpha
    for i in range(g):
        pr = jnp.exp(logits[i] - m_new)
        l = l + jnp.sum(pr, axis=1, keepdims=True)
        acc = acc + _dot(pr.astype(BF16), v_refs[i][...].reshape(MIX, PAGE).astype(BF16), NT)
    m_s[...], l_s[...], acc_s[...] = m_new, l, acc

    @pl.when(p == pl.num_programs(1) - 1)
    def _():
        o_ref[...] = jnp.sum(jnp.where(headmask, acc / l, 0.0), axis=0, keepdims=True)


def _fox_decode(layer, q, kn, vn, lfn, cache_kt, cache_vt, cache_lft, page_table):
    bsz, n_pages = page_table.shape
    g = DECODE_PAGES_PER_STEP
    assert n_pages % g == 0

    def page_spec(block, i):
        return pl.BlockSpec((None, None) + block,
                            lambda b, p, pt: (layer, pt[b, n_pages - 1 - (p * g + i)]) + (0,) * len(block))

    tok = lambda r, w: pl.BlockSpec((None, r, w), lambda b, p, pt: (b, 0, 0))
    in_specs = [tok(1, MIX), tok(1, MIX), tok(1, MIX), tok(N_HEADS, 1)]
    in_specs += [page_spec((N_HEADS, HEAD_DIM, PAGE), i) for i in range(g)]
    in_specs += [page_spec((N_HEADS, HEAD_DIM, PAGE), i) for i in range(g)]
    in_specs += [page_spec((N_HEADS, PAGE), i) for i in range(g)]
    grid_spec = pltpu.PrefetchScalarGridSpec(
        num_scalar_prefetch=1,
        grid=(bsz, n_pages // g),
        in_specs=in_specs,
        out_specs=tok(1, MIX),
        scratch_shapes=[pltpu.VMEM((N_HEADS, 1), F32), pltpu.VMEM((N_HEADS, 1), F32),
                        pltpu.VMEM((N_HEADS, MIX), F32), pltpu.VMEM((N_HEADS, 1), F32)],
    )
    return pl.pallas_call(
        functools.partial(_fox_decode_kernel, pages_per_step=g),
        grid_spec=grid_spec,
        out_shape=jax.ShapeDtypeStruct((bsz, 1, MIX), F32),
        compiler_params=_params(("parallel", "arbitrary")),
    )(page_table, q, kn, vn, lfn, *([cache_kt] * g), *([cache_vt] * g), *([cache_lft] * g))


def _rwkv_prep_kernel(rw_ref, prev_ref, sh0_ref, mu_ref, w0_ref, w2_ref, a0_ref, a2_ref, g2_ref,
                      kk_ref, ka_ref, rk_ref, ones_ref,
                      r_out, lw_out, k_out, v_out, kk_out, b_out, g_out, bonus_out):
    i = pl.program_id(1)
    rw = rw_ref[...]
    tm = rw.shape[0]
    first = jnp.where(i == 0, sh0_ref[...], prev_ref[SUBLANE - 1:SUBLANE, :])
    trow = lax.broadcasted_iota(jnp.int32, (tm, 1), 0)
    prev = jnp.where(trow == 0, first, pltpu.roll(rw, 1, 0))
    z = rw + (prev - rw) * mu_ref[...]
    r = z[:, 0:MIX]
    k = z[:, MIX:2 * MIX]
    v = z[:, 2 * MIX:3 * MIX]
    xwa = z[:, 3 * MIX:3 * MIX + 2 * LORA_WA]
    xg = z[:, 3 * MIX + 2 * LORA_WA:]
    ones_bd = ones_ref[...]
    w = -_softplus(-(w0_ref[...] + _dot(jnp.tanh(xwa).astype(BF16), w2_ref[...]))) - 0.5
    lw_out[...] = -jnp.exp(w)
    a = jax.nn.sigmoid(a0_ref[...] + _dot(xwa.astype(BF16), a2_ref[...]))
    g_out[...] = _dot(jax.nn.sigmoid(xg).astype(BF16), g2_ref[...])
    kk = k * kk_ref[...]
    kk = kk / jnp.maximum(jnp.sqrt(_segsum(kk * kk, ones_bd)), 1e-12)
    k2 = k * (1.0 + (a - 1.0) * ka_ref[...])
    r_out[...] = r.astype(BF16)
    k_out[...] = k2.astype(BF16)
    v_out[...] = v.astype(BF16)
    kk_out[...] = kk.astype(BF16)
    b_out[...] = (kk * a).astype(BF16)
    bonus_out[...] = _segsum(r * k2 * rk_ref[...], ones_bd) * v


def _rwkv_prep(rwp, shift0, prm, bsz, tp):
    tm = _tile(tp, 544)
    nb = tm // SUBLANE
    small = lambda a: pl.BlockSpec(a.shape, lambda b, i: (0, 0))
    consts = [prm["mu"], prm["w0"], prm["w2"], prm["a0"], prm["a2"], prm["g2"], prm["k_k"], prm["k_a"],
              prm["r_k"], prm["ones_bd"]]
    rw3 = rwp.reshape(bsz, tp, RW_PAD)
    outs = pl.pallas_call(
        _rwkv_prep_kernel,
        grid=(bsz, tp // tm),
        in_specs=[
            pl.BlockSpec((None, tm, RW_PAD), lambda b, i: (b, i, 0)),
            pl.BlockSpec((None, SUBLANE, RW_PAD), lambda b, i: (b, jnp.maximum(i * nb - 1, 0), 0)),
            pl.BlockSpec((None, 1, RW_PAD), lambda b, i: (b, 0, 0)),
        ] + [small(c) for c in consts],
        out_specs=[pl.BlockSpec((None, tm, MIX), lambda b, i: (b, i, 0))] * 8,
        out_shape=[jax.ShapeDtypeStruct((bsz, tp, MIX), dt) for dt in (BF16, F32, BF16, BF16, BF16, BF16, F32, F32)],
        compiler_params=_params(("parallel", "parallel")),
    )(rw3, rw3, shift0, *consts)
    return outs


def _rwkv_scan_kernel(r_ref, lw_ref, k_ref, v_ref, kk_ref, b_ref, s0_ref, o_ref, s_out, s_s, *, chunk, t_valid):
    c = pl.program_id(1)
    n = chunk

    @pl.when(c == 0)
    def _():
        s_s[...] = s0_ref[...]

    nseq = r_ref.shape[0]
    trow = lax.broadcasted_iota(jnp.int32, (n, 1), 0) + c * n
    valid = trow < t_valid
    row = lax.broadcasted_iota(jnp.int32, (n, n), 0)
    col = lax.broadcasted_iota(jnp.int32, (n, n), 1)
    incl = (row >= col).astype(F32)
    eye = (row == col).astype(F32)
    row2 = lax.broadcasted_iota(jnp.int32, (n, 2 * n), 0)
    col2 = lax.broadcasted_iota(jnp.int32, (n, 2 * n), 1)
    col2 = jnp.where(col2 >= n, col2 - n, col2)
    strict2 = row2 > col2
    incl2 = row2 >= col2

    a_m, b_m, bk, v_h, wl_h, s_all = [], [], [], [], [], []
    for i in range(nseq):
        lw = jnp.where(valid, lw_ref[i], 0.0)
        kk = jnp.where(valid, kk_ref[i].astype(F32), 0.0)
        bb = jnp.where(valid, b_ref[i].astype(F32), 0.0)
        kx = jnp.where(valid, k_ref[i].astype(F32), 0.0)
        vx = jnp.where(valid, v_ref[i].astype(F32), 0.0)
        rx = r_ref[i].astype(F32)
        ci = _dot_exact_lhs(incl, lw)
        kkd = kk * jnp.exp(ci - lw)
        rd = rx * jnp.exp(ci)
        einv = jnp.exp(-ci)
        bd = bb * einv
        kd = kx * einv
        cl = ci[n - 1:n, :]
        wl = jnp.exp(cl)
        dl = jnp.exp(cl - ci)
        bdw = bb * dl
        kdw = kx * dl
        for h in range(N_HEADS):
            sl = slice(h * HEAD_DIM, (h + 1) * HEAD_DIM)
            a_m.append(jnp.concatenate([kkd[:, sl], rd[:, sl]], axis=0))
            b_m.append(jnp.concatenate([bd[:, sl], kd[:, sl]], axis=0))
            bk.append(jnp.concatenate([bdw[:, sl], kdw[:, sl]], axis=0))
            v_h.append(vx[:, sl])
            wl_h.append(wl[:, sl])
            s_all.append(s_s[i, h])
    ch = range(nseq * N_HEADS)
    pm = [_dot1(a_m[j], b_m[j], NT) for j in ch]
    p_top = [jnp.where(strict2, pm[j][:n, :], 0.0) for j in ch]
    p_bot = [jnp.where(incl2, pm[j][n:, :], 0.0) for j in ch]
    gm = [_dot1(a_m[j], s_all[j], NT) for j in ch]
    x = [-p_top[j][:, :n] for j in ch]
    tinv = [eye + x[j] for j in ch]
    span = 2
    while span < n:
        x = [_dot1(x[j], x[j]) for j in ch]
        tinv = [tinv[j] + _dot1(tinv[j], x[j]) for j in ch]
        span *= 2
    rhs = [gm[j][:n, :] + _dot1(p_top[j][:, n:], v_h[j]) for j in ch]
    u = [-_dot1(tinv[j], rhs[j]) for j in ch]
    uv = [jnp.concatenate([u[j], v_h[j]], axis=0) for j in ch]
    outs = [gm[j][n:, :] + _dot1(p_bot[j], uv[j]) for j in ch]
    s_new = [s_all[j] * wl_h[j] + _dot1(uv[j], bk[j], TN) for j in ch]
    for i in range(nseq):
        for h in range(N_HEADS):
            s_s[i, h] = s_new[i * N_HEADS + h]
        o_ref[i] = jnp.concatenate(outs[i * N_HEADS:(i + 1) * N_HEADS], axis=1)

    @pl.when(c == pl.num_programs(1) - 1)
    def _():
        s_out[...] = s_s[...]


def _rwkv_scan(r, lw, k, v, kk, b, s0, bsz, tp, t_valid, chunk):
    ns = RWKV_SEQS_PER_STEP
    assert bsz % ns == 0
    blk = pl.BlockSpec((ns, chunk, MIX), lambda bi, c: (bi, c, 0))
    st = pl.BlockSpec((ns, N_HEADS, HEAD_DIM, HEAD_DIM), lambda bi, c: (bi, 0, 0, 0))
    o, s_out = pl.pallas_call(
        functools.partial(_rwkv_scan_kernel, chunk=chunk, t_valid=t_valid),
        grid=(bsz // ns, tp // chunk),
        in_specs=[blk] * 6 + [st],
        out_specs=[blk, st],
        out_shape=[jax.ShapeDtypeStruct((bsz, tp, MIX), F32),
                   jax.ShapeDtypeStruct((bsz, N_HEADS, HEAD_DIM, HEAD_DIM), F32)],
        scratch_shapes=[pltpu.VMEM((ns, N_HEADS, HEAD_DIM, HEAD_DIM), F32)],
        compiler_params=_params(("parallel", "arbitrary")),
    )(r, lw, k, v, kk, b, s0)
    return o.reshape(bsz * tp, MIX), s_out


def _s5_kernel(u_ref, x0_ref, a_ref, bm_ref, cm_ref, d_ref, gw_ref, gb_ref, o_ref, xo_ref, buf_s, st_s,
               *, t_valid):
    c = pl.program_id(0)
    bsz, tc, _ = u_ref.shape
    tcp = buf_s.shape[1] // bsz
    half = S5_LANES // 2
    hc = MIX // 2
    nblk = S5_LANES // LANE
    hblk = half // LANE

    @pl.when(c == 0)
    def _():
        st_s[...] = x0_ref[...]

    u2 = u_ref[...].reshape(bsz * tc, MIX)
    ub = u2.astype(BF16)
    for part in range(2):
        for j in range(2):
            col0 = part * S5_LANES + j * half
            res = _dot(ub[:, j * hc:(j + 1) * hc], bm_ref[j * hc:(j + 1) * hc, col0:col0 + half])
            for k in range(hblk):
                for b in range(bsz):
                    buf_s[col0 // LANE + k, b * tcp:b * tcp + tc, :] = res[b * tc:(b + 1) * tc,
                                                                           k * LANE:(k + 1) * LANE]

    gb = max(1, 2 * nblk // bsz)
    for g0 in range(0, nblk, gb):
        blks = list(range(g0, g0 + gb))
        lane = lambda k: slice(k * LANE, (k + 1) * LANE)
        ar = [jnp.broadcast_to(a_ref[:, lane(k)], (bsz, LANE)) for k in blks]
        ai = [jnp.broadcast_to(a_ref[:, lane(nblk + k)], (bsz, LANE)) for k in blks]

        def body(t, carry, blks=blks, ar=ar, ai=ai):
            rows = pl.ds(t, bsz, stride=tcp)
            out = []
            for n, k in enumerate(blks):
                xr, xi = carry[n]
                nxr = ar[n] * xr - ai[n] * xi + buf_s[k, rows, :]
                nxi = ar[n] * xi + ai[n] * xr + buf_s[nblk + k, rows, :]
                buf_s[k, rows, :] = nxr
                buf_s[nblk + k, rows, :] = nxi
                out.append((nxr, nxi))
            return tuple(out)

        init = tuple((st_s[:, lane(k)], st_s[:, lane(nblk + k)]) for k in blks)
        fin = lax.fori_loop(0, tc, body, init)
        for n, k in enumerate(blks):
            st_s[:, lane(k)] = fin[n][0]
            st_s[:, lane(nblk + k)] = fin[n][1]

    c_valid, t_in = divmod(t_valid - 1, tc)

    @pl.when(c == c_valid)
    def _():
        for k in range(2 * nblk):
            xo_ref[:, k * LANE:(k + 1) * LANE] = buf_s[k, pl.ds(t_in, bsz, stride=tcp), :]

    ys = []
    for j in range(2):
        acc = None
        for part in range(2):
            col0 = part * S5_LANES + j * half
            xs = jnp.concatenate(
                [jnp.concatenate([buf_s[col0 // LANE + k, b * tcp:b * tcp + tc, :] for b in range(bsz)], axis=0)
                 for k in range(hblk)], axis=1)
            d = _dot(xs.astype(BF16), cm_ref[col0:col0 + half, j * hc:(j + 1) * hc])
            acc = d if acc is None else acc + d
        ys.append(acc)
    y = jnp.concatenate(ys, axis=1) + d_ref[...] * u2
    y = 0.5 * y * (1.0 + jnp.tanh(0.7978845608028654 * (y + 0.044715 * (y * y * y))))
    gate = jax.nn.sigmoid(_dot(y.astype(BF16), gw_ref[...]) + gb_ref[...])
    o_ref[...] = (y * gate).reshape(bsz, tc, MIX).astype(o_ref.dtype)


def _s5(qkvu, x0, prm, bsz, tp, t_valid, tc):
    n2 = 2 * S5_LANES
    full = lambda a: pl.BlockSpec(a.shape, lambda c: (0,) * a.ndim)
    consts = [prm["s5_a"], prm["s5_bm"], prm["s5_cm"], prm["s5_d"], prm["s5_gw"], prm["s5_gb"]]
    o, xo = pl.pallas_call(
        functools.partial(_s5_kernel, t_valid=t_valid),
        grid=(tp // tc,),
        in_specs=[pl.BlockSpec((bsz, tc, MIX), lambda c: (0, c, 3)), full(x0)] + [full(a) for a in consts],
        out_specs=[pl.BlockSpec((bsz, tc, MIX), lambda c: (0, c, 0)), pl.BlockSpec((bsz, n2), lambda c: (0, 0))],
        out_shape=[jax.ShapeDtypeStruct((bsz, tp, MIX), BF16), jax.ShapeDtypeStruct((bsz, n2), F32)],
        scratch_shapes=[pltpu.VMEM((n2 // LANE, bsz * (tc + S5_SLAB_PAD), LANE), F32),
                        pltpu.VMEM((bsz, n2), F32)],
        compiler_params=_params(("arbitrary",)),
    )(qkvu.reshape(bsz, tp, QKVU_COLS), x0, *consts)
    return o.reshape(bsz * tp, MIX), xo


def _merge_kernel(x_ref, hn_ref, oa_ref, ob_ref, g_ref, bonus_ref, oc_ref, lw_ref, lb_ref, ones_ref,
                  wg_ref, wb_ref, wo_ref, o_ref):
    ones_bd = ones_ref[...]
    o = ob_ref[...]
    mu = _segsum(o, ones_bd) * (1.0 / HEAD_DIM)
    dlt = o - mu
    var = _segsum(dlt * dlt, ones_bd) * (1.0 / HEAD_DIM)
    ob = (dlt * lax.rsqrt(var + LNX_EPS) * lw_ref[...] + lb_ref[...] + bonus_ref[...]) * g_ref[...]
    hn = hn_ref[...]
    branches = (oa_ref[...].astype(BF16), ob.astype(BF16), oc_ref[...].astype(BF16))
    merged = None
    for i, br in enumerate(branches):
        gate = jax.nn.sigmoid(_dot(hn, wg_ref[:, i * D_MODEL:(i + 1) * D_MODEL]))
        term = gate * _dot(br, wb_ref[i * MIX:(i + 1) * MIX, :])
        merged = term if merged is None else merged + term
    o_ref[...] = x_ref[...] + _dot(merged.astype(BF16), wo_ref[...])


def _merge(x, hn, oa, ob, g, bonus, oc, prm):
    rows = x.shape[0]
    tm = _tile(rows, 512)
    row = lambda w: pl.BlockSpec((tm, w), lambda i: (i, 0))
    full = lambda a: pl.BlockSpec(a.shape, lambda i: (0, 0))
    consts = [prm["lnx_w"], prm["lnx_b"], prm["ones_bd"], prm["w_gates"], prm["w_branch"], prm["w_out"]]
    return pl.pallas_call(
        _merge_kernel,
        grid=(rows // tm,),
        in_specs=[row(D_MODEL), row(D_MODEL), row(MIX), row(MIX), row(MIX), row(MIX), row(MIX)]
        + [full(c) for c in consts],
        out_specs=row(D_MODEL),
        out_shape=jax.ShapeDtypeStruct((rows, D_MODEL), F32),
        compiler_params=_params(("parallel",)),
    )(x, hn, oa, ob, g, bonus, oc, *consts)


def _layer_params(l, p):
    bf = lambda a: a.astype(BF16)
    row = lambda a, w=None: (a if w is None else jnp.pad(a, (0, w - a.shape[0]))).reshape(1, -1).astype(F32)
    w_in = p["w_in"][l]
    o_f = 3 * MIX
    o_rw = o_f + N_HEADS
    o_u = o_rw + RW_COLS
    o_g = o_u + MIX
    w_qkvu = jnp.concatenate(
        [w_in[:, :o_f], w_in[:, o_u:o_g], jnp.pad(w_in[:, o_f:o_rw], ((0, 0), (0, LANE - N_HEADS)))], axis=1)
    w_rw = jnp.pad(w_in[:, o_rw:o_u], ((0, 0), (0, RW_PAD - RW_COLS)))
    w_gates = w_in[:, o_g:]

    ar = p["s5_a_re"][l].astype(F32)
    ai = p["s5_a_im"][l].astype(F32)
    dt = jnp.exp(p["s5_log_dt"][l].astype(F32))[:, None]
    mag = jnp.exp(dt * ar)
    abar_re = mag * jnp.cos(dt * ai)
    abar_im = mag * jnp.sin(dt * ai)
    den = ar * ar + ai * ai
    nr, ni = abar_re - 1.0, abar_im
    z_re = (nr * ar + ni * ai) / den
    z_im = (ni * ar - nr * ai) / den
    b_re = p["s5_b_re"][l].astype(F32)
    b_im = p["s5_b_im"][l].astype(F32)
    bb_re = z_re[..., None] * b_re - z_im[..., None] * b_im
    bb_im = z_re[..., None] * b_im + z_im[..., None] * b_re
    eye = jnp.eye(S5_GROUPS, dtype=F32)
    blk_in = lambda m: jnp.einsum("gpc,gh->gchp", m, eye).reshape(MIX, S5_LANES)
    blk_out = lambda m: jnp.einsum("gcp,gh->gphc", m, eye).reshape(S5_LANES, MIX)
    s5_bm = jnp.concatenate([blk_in(bb_re), blk_in(bb_im)], axis=1)
    s5_cm = jnp.concatenate([blk_out(p["s5_c_re"][l].astype(F32)), -blk_out(p["s5_c_im"][l].astype(F32))], axis=0)

    hd = lax.broadcasted_iota(jnp.int32, (MIX, MIX), 0) // HEAD_DIM
    hd_t = lax.broadcasted_iota(jnp.int32, (MIX, MIX), 1) // HEAD_DIM
    return dict(
        ffn1_norm=row(p["ffn1_norm"][l]), ffn1_w_in=bf(p["ffn1_w_in"][l]), ffn1_w_out=bf(p["ffn1_w_out"][l]),
        ffn2_norm=row(p["ffn2_norm"][l]), ffn2_w_in=bf(p["ffn2_w_in"][l]), ffn2_w_out=bf(p["ffn2_w_out"][l]),
        mix_norm=row(p["mix_norm"][l]),
        w_qkvu=bf(w_qkvu), w_rw=bf(w_rw), w_gates=bf(w_gates),
        q_gain=row(jnp.tile(p["fox_q_gain"][l], N_HEADS)), k_gain=row(jnp.tile(p["fox_k_gain"][l], N_HEADS)),
        b_f=row(p["fox_b_f"][l], LANE),
        ones_bd=(hd == hd_t).astype(BF16),
        mu=row(p["rwkv_mu"][l], RW_PAD), w0=row(p["rwkv_w0"][l]), a0=row(p["rwkv_a0"][l]),
        w2=bf(jnp.pad(p["rwkv_w2"][l], ((0, LORA_WA), (0, 0)))),
        a2=bf(jnp.pad(p["rwkv_a2"][l], ((LORA_WA, 0), (0, 0)))),
        g2=bf(jnp.pad(p["rwkv_g2"][l], ((0, RW_PAD - RW_COLS), (0, 0)))),
        k_k=row(p["rwkv_k_k"][l]), k_a=row(p["rwkv_k_a"][l]), r_k=row(p["rwkv_r_k"][l].reshape(-1)),
        lnx_w=row(p["rwkv_lnx_w"][l]), lnx_b=row(p["rwkv_lnx_b"][l]),
        s5_a=jnp.concatenate([abar_re.reshape(1, -1), abar_im.reshape(1, -1)], axis=1),
        s5_bm=bf(s5_bm), s5_cm=bf(s5_cm), s5_d=row(p["s5_d"][l].reshape(-1)),
        s5_gw=bf(p["s5_glu_w"][l]), s5_gb=row(p["s5_glu_b"][l]),
        w_branch=bf(p["w_branch"][l]), w_out=bf(p["w_out"][l]),
    )


def _mixer_inputs(x, prm, bsz, tp, t_valid):
    x, hn = _ffn(x, prm["ffn1_norm"], prm["ffn1_w_in"], prm["ffn1_w_out"], prm["mix_norm"], True)
    qkvu = _matmul(hn, prm["w_qkvu"], QKVU_COLS)
    rwp = _matmul(hn, prm["w_rw"], RW_PAD)
    qn, kn, lf, lfp, kv = _qk_prep(qkvu, prm["q_gain"], prm["k_gain"], prm["b_f"], prm["ones_bd"], bsz, tp, t_valid)
    return x, qkvu, rwp, hn, qn, kn, lf, lfp, kv


def _layer_tail(x, oa, qkvu, rwp, hn, prm, shift0, s0, x0, bsz, tp, t_valid, chunk, tc):
    r, lw, k2, v, kk, b, g, bonus = _rwkv_prep(rwp, shift0, prm, bsz, tp)
    ob, s_out = _rwkv_scan(r, lw, k2, v, kk, b, s0, bsz, tp, t_valid, chunk)
    oc, xo = _s5(qkvu, x0, prm, bsz, tp, t_valid, tc)
    flat = lambda a: a.reshape(bsz * tp, MIX)
    x = _merge(x, hn, oa, ob, flat(g), flat(bonus), oc, prm)
    x, _ = _ffn(x, prm["ffn2_norm"], prm["ffn2_w_in"], prm["ffn2_w_out"], prm["ffn2_norm"], False)
    return x, s_out, xo


def kernel(x_prompt, x_sample, cache_k, cache_v, cache_logf, page_table, state_rwkv, state_shift, state_s5_re, state_s5_im, meta_tokens, ffn1_norm, ffn1_w_in, ffn1_w_out, mix_norm, w_in, fox_b_f, fox_q_gain, fox_k_gain, rwkv_mu, rwkv_w0, rwkv_w2, rwkv_a0, rwkv_a2, rwkv_g2, rwkv_k_k, rwkv_k_a, rwkv_r_k, rwkv_lnx_w, rwkv_lnx_b, s5_a_re, s5_a_im, s5_log_dt, s5_b_re, s5_b_im, s5_c_re, s5_c_im, s5_d, s5_glu_w, s5_glu_b, w_branch, w_out, ffn2_norm, ffn2_w_in, ffn2_w_out):
    p = dict(ffn1_norm=ffn1_norm, ffn1_w_in=ffn1_w_in, ffn1_w_out=ffn1_w_out, mix_norm=mix_norm,
             w_in=w_in, fox_b_f=fox_b_f, fox_q_gain=fox_q_gain, fox_k_gain=fox_k_gain,
             rwkv_mu=rwkv_mu, rwkv_w0=rwkv_w0, rwkv_w2=rwkv_w2, rwkv_a0=rwkv_a0, rwkv_a2=rwkv_a2,
             rwkv_g2=rwkv_g2, rwkv_k_k=rwkv_k_k, rwkv_k_a=rwkv_k_a, rwkv_r_k=rwkv_r_k,
             rwkv_lnx_w=rwkv_lnx_w, rwkv_lnx_b=rwkv_lnx_b, s5_a_re=s5_a_re, s5_a_im=s5_a_im,
             s5_log_dt=s5_log_dt, s5_b_re=s5_b_re, s5_b_im=s5_b_im, s5_c_re=s5_c_re, s5_c_im=s5_c_im,
             s5_d=s5_d, s5_glu_w=s5_glu_w, s5_glu_b=s5_glu_b, w_branch=w_branch, w_out=w_out,
             ffn2_norm=ffn2_norm, ffn2_w_in=ffn2_w_in, ffn2_w_out=ffn2_w_out)
    depth = w_in.shape[0]
    prms = [_layer_params(l, p) for l in range(depth)]

    bp, seq, _ = x_prompt.shape
    t_p = seq + N_META
    tp = -(-t_p // SEQ_ALIGN) * SEQ_ALIGN
    meta = jnp.broadcast_to(meta_tokens.astype(F32)[None], (bp, N_META, D_MODEL))
    xp = jnp.concatenate([meta, x_prompt], axis=1)
    xp = jnp.pad(xp, ((0, 0), (0, tp - t_p), (0, 0))).reshape(bp * tp, D_MODEL)
    z_shift = jnp.zeros((bp, 1, RW_PAD), F32)
    z_s = jnp.zeros((bp, N_HEADS, HEAD_DIM, HEAD_DIM), F32)
    z_x = jnp.zeros((bp, 2 * S5_LANES), F32)
    seqs = lambda a, w: a.reshape(bp, tp, w)
    pk, pv, plf, prw, psh, pre, pim = [], [], [], [], [], [], []
    for l in range(depth):
        prm = prms[l]
        xp, qkvu, rwp, hn, qn, kn, lf, lfp, kv = _mixer_inputs(xp, prm, bp, tp, t_p)
        oa = _fox_attention(qn, kn, qkvu, lfp, bp, tp)
        xp, s_out, xo = _layer_tail(xp, oa, qkvu, rwp, hn, prm, z_shift, z_s, z_x, bp, tp, t_p,
                                    RWKV_CHUNK, S5_CHUNK)
        if kv is None:
            kv = (seqs(kn, MIX)[:, :t_p], seqs(qkvu, QKVU_COLS)[:, :t_p, 2 * MIX:3 * MIX])
        pk.append(kv[0].reshape(bp, t_p, N_HEADS, HEAD_DIM))
        pv.append(kv[1].reshape(bp, t_p, N_HEADS, HEAD_DIM))
        plf.append(seqs(lf, LANE)[:, :t_p, :N_HEADS])
        prw.append(s_out)
        psh.append(seqs(rwp, RW_PAD)[:, t_p - 1, :RW_COLS])
        pre.append(xo[:, :S5_LANES].reshape(bp, S5_GROUPS, S5_STATE))
        pim.append(xo[:, S5_LANES:].reshape(bp, S5_GROUPS, S5_STATE))
    y_prompt = seqs(xp, D_MODEL)[:, N_META:t_p]

    bs = x_sample.shape[0]
    ts = SUBLANE
    xs = jnp.pad(x_sample, ((0, 0), (0, ts - 1), (0, 0))).reshape(bs * ts, D_MODEL)
    ck =cache_k.transpose(0, 1, 3, 4, 2)
    cv = cache_v.transpose(0, 1, 3, 4, 2)
    clf = cache_logf.transpose(0, 1, 3, 2)
    tok0 = lambda a, w: a.reshape(bs, ts, w)[:, 0]
    sk, sv, slf, srw, ssh, sre, sim = [], [], [], [], [], [], []
    for l in range(depth):
        prm = prms[l]
        xs, qkvu, rwp, hn, qn, kn, lf, lfp, _ = _mixer_inputs(xs, prm, bs, ts, 1)
        q_new = tok0(qn, MIX).astype(F32)
        k_new = tok0(kn, MIX)
        v_new = tok0(qkvu, QKVU_COLS)[:, 2 * MIX:3 * MIX]
        lf_new = tok0(lf, LANE)[:, :N_HEADS]
        oa = _fox_decode(l, q_new[:, None], k_new[:, None], v_new[:, None], lf_new[:, :, None], ck, cv, clf,
                         page_table)
        oa = jnp.pad(oa, ((0, 0), (0, ts - 1), (0, 0))).reshape(bs * ts, MIX)
        shift0 = jnp.pad(state_shift[l].astype(F32), ((0, 0), (0, RW_PAD - RW_COLS)))[:, None]
        x0 = jnp.concatenate([state_s5_re[l].reshape(bs, S5_LANES), state_s5_im[l].reshape(bs, S5_LANES)],
                             axis=1).astype(F32)
        xs, s_out, xo = _layer_tail(xs, oa, qkvu, rwp, hn, prm, shift0, state_rwkv[l].astype(F32), x0,
                                    bs, ts, 1, ts, ts)
        sk.append(k_new.reshape(bs, 1, N_HEADS, HEAD_DIM))
        sv.append(v_new.reshape(bs, 1, N_HEADS, HEAD_DIM))
        slf.append(lf_new.reshape(bs, 1, N_HEADS))
        srw.append(s_out)
        ssh.append(tok0(rwp, RW_PAD)[:, :RW_COLS])
        sre.append(xo[:, :S5_LANES].reshape(bs, S5_GROUPS, S5_STATE))
        sim.append(xo[:, S5_LANES:].reshape(bs, S5_GROUPS, S5_STATE))
    y_sample = tok0(xs, D_MODEL).reshape(bs, 1, D_MODEL)

    st = lambda xs_: jnp.stack(xs_, axis=0)
    return (y_prompt, y_sample, st(pk), st(pv), st(plf), st(prw), st(psh), st(pre), st(pim),
            st(sk), st(sv), st(slf), st(srw), st(ssh), st(sre), st(sim))
```

```python
import functools

import jax
import jax.numpy as jnp
from jax import lax
from jax.experimental import pallas as pl
from jax.experimental.pallas import tpu as pltpu

F32 = jnp.float32
BF16 = jnp.bfloat16

D_MODEL = 1024
N_META = 16
HEAD_DIM = 64
MIX = D_MODEL // 2
N_HEADS = MIX // HEAD_DIM
D_FF = 2816
RW_COLS = 1824
RW_PAD = 1920
LORA_WA = 64
LORA_G = 160
S5_GROUPS = 32
S5_GROUP = 16
S5_STATE = 64
S5_LANES = S5_GROUPS * S5_STATE
QKVU_COLS = 4 * MIX + 128
PAGE = 128
NORM_EPS = 1e-6
LNX_EPS = 64e-5

LANE = 128
SEQ_ALIGN = 128
FF_TILE = 1408
FF_ROWS = 512
ATT_Q_TILE = 512
RWKV_CHUNK = 64
RWKV_SEQS_PER_STEP = 2
S5_CHUNK = 64
S5_SLAB_PAD = 8
SUBLANE = 8
DECODE_PAGES_PER_STEP = 32
VMEM_LIMIT = 56 * 1024 * 1024

NN = (((1,), (0,)), ((), ()))
NT = (((1,), (1,)), ((), ()))
TN = (((0,), (0,)), ((), ()))


def _params(sem):
    return pltpu.CompilerParams(dimension_semantics=sem, vmem_limit_bytes=VMEM_LIMIT)


def _tile(n, target):
    best = None
    for t in range(8, min(n, target) + 1, 8):
        if n % t == 0:
            best = t
    return best or n


def _dot(a, b, dn=NN):
    return lax.dot_general(a, b, dn, preferred_element_type=F32)


def _split2(x):
    hi = x.astype(BF16)
    return hi, (x - hi.astype(F32)).astype(BF16)


def _dot1(a, b, dn=NN):
    return _dot(a.astype(BF16), b.astype(BF16), dn)


def _dot_exact_lhs(a01, b):
    a = a01.astype(BF16)
    b1 = b.astype(BF16)
    r1 = b - b1.astype(F32)
    b2 = r1.astype(BF16)
    b3 = (r1 - b2.astype(F32)).astype(BF16)
    return _dot(a, b1) + _dot(a, b2) + _dot(a, b3)


def _dot_exact_rhs(a, b01):
    b = b01.astype(BF16)
    a1 = a.astype(BF16)
    r1 = a - a1.astype(F32)
    a2 = r1.astype(BF16)
    a3 = (r1 - a2.astype(F32)).astype(BF16)
    return _dot(a1, b) + _dot(a2, b) + _dot(a3, b)


def _segsum(x, ones_bd):
    hi, lo = _split2(x)
    return _dot(hi, ones_bd) + _dot(lo, ones_bd)


def _softplus(x):
    return jnp.maximum(x, 0.0) + jnp.log1p(jnp.exp(-jnp.abs(x)))


def _rms(x, g):
    ms = jnp.mean(x * x, axis=-1, keepdims=True)
    return x * lax.rsqrt(ms + NORM_EPS) * g


def _ffn_kernel(x_ref, g_ref, wi_ref, wo_ref, gn_ref, *rest, emit_hn):
    o_ref = rest[0]
    tm = x_ref.shape[0]
    nf = D_FF // FF_TILE
    for r0 in range(0, tm, tm // 2):
        rows = slice(r0, r0 + tm // 2)
        x = x_ref[rows, :]
        xn = _rms(x, g_ref[...]).astype(BF16)
        acc = None
        for j in range(nf):
            gate = _dot(xn, wi_ref[:, j * FF_TILE:(j + 1) * FF_TILE])
            up = _dot(xn, wi_ref[:, D_FF + j * FF_TILE:D_FF + (j + 1) * FF_TILE])
            act = (gate * jax.nn.sigmoid(gate) * up).astype(BF16)
            part = _dot(act, wo_ref[j * FF_TILE:(j + 1) * FF_TILE, :])
            acc = part if acc is None else acc + part
        y = x + 0.5 * acc
        o_ref[rows, :] = y
        if emit_hn:
            rest[1][rows, :] = _rms(y, gn_ref[...]).astype(BF16)


def _ffn(x, g, w_in, w_out, g_next, emit_hn):
    rows = x.shape[0]
    tm = _tile(rows, FF_ROWS)
    row = pl.BlockSpec((tm, D_MODEL), lambda i: (i, 0))
    once = lambda a: pl.BlockSpec(a.shape, lambda i: (0, 0), pipeline_mode=pl.Buffered(1))
    out_shape = [jax.ShapeDtypeStruct((rows, D_MODEL), F32)]
    out_specs = [row]
    if emit_hn:
        out_shape.append(jax.ShapeDtypeStruct((rows, D_MODEL), BF16))
        out_specs.append(row)
    res = pl.pallas_call(
        functools.partial(_ffn_kernel, emit_hn=emit_hn),
        grid=(rows // tm,),
        in_specs=[row, once(g), once(w_in), once(w_out), once(g_next)],
        out_specs=out_specs,
        out_shape=out_shape,
        compiler_params=_params(("parallel",)),
    )(x, g, w_in, w_out, g_next)
    return res if emit_hn else (res[0], None)


def _matmul_kernel(a_ref, b_ref, o_ref):
    o_ref[...] = _dot(a_ref[...], b_ref[...])


def _matmul(a, b, tn):
    rows, k = a.shape
    n = b.shape[1]
    tm = _tile(rows, 1024)
    return pl.pallas_call(
        _matmul_kernel,
        grid=(rows // tm, n // tn),
        in_specs=[pl.BlockSpec((tm, k), lambda i, j: (i, 0)), pl.BlockSpec((k, tn), lambda i, j: (0, j))],
        out_specs=pl.BlockSpec((tm, tn), lambda i, j: (i, j)),
        out_shape=jax.ShapeDtypeStruct((rows, n), F32),
        compiler_params=_params(("parallel", "arbitrary")),
    )(a, b)


def _qk_prep_kernel(q_ref, k_ref, v_ref, f_ref, qg_ref, kg_ref, bf_ref, ones_ref, qn_ref, kn_ref, lf_ref, lfp_ref,
                    *kv_out):
    ones_bd = ones_ref[...]

    def headnorm(x, g):
        ms = _segsum(x * x, ones_bd) * (1.0 / HEAD_DIM)
        return x * lax.rsqrt(ms + NORM_EPS) * g

    qn_ref[...] = (headnorm(q_ref[...], qg_ref[...]) * (HEAD_DIM ** -0.5)).astype(BF16)
    kn = headnorm(k_ref[...], kg_ref[...])
    kn_ref[...] = kn
    if kv_out:
        kv_out[0][...] = kn
        kv_out[1][...] = v_ref[...]
    lf = -_softplus(-(f_ref[...] + bf_ref[...]))
    lf_ref[...] = lf
    lane = lax.broadcasted_iota(jnp.int32, lf.shape, 1)
    for hp in range(N_HEADS // 2):
        a = lf[:, 2 * hp:2 * hp + 1]
        b = lf[:, 2 * hp + 1:2 * hp + 2]
        lfp_ref[:, hp * LANE:(hp + 1) * LANE] = jnp.where(lane == 0, a, jnp.where(lane == 1, b, 0.0))


def _qk_prep(qkvu, qg, kg, bf, ones_bd, bsz, tp, t_valid):
    tm = _tile(tp, 544)
    emit_kv = t_valid % SUBLANE == 0
    small = lambda w: pl.BlockSpec((1, w), lambda b, i: (0, 0))
    blk = lambda w, j=0: pl.BlockSpec((None, tm, w), lambda b, i: (b, i, j))
    out_specs = [blk(MIX), blk(MIX), blk(LANE), blk(MIX)]
    out_shape = [jax.ShapeDtypeStruct((bsz, tp, MIX), BF16), jax.ShapeDtypeStruct((bsz, tp, MIX), F32),
                 jax.ShapeDtypeStruct((bsz, tp, LANE), F32), jax.ShapeDtypeStruct((bsz, tp, MIX), F32)]
    if emit_kv:
        out_specs += [blk(MIX), blk(MIX)]
        out_shape += [jax.ShapeDtypeStruct((bsz, t_valid, MIX), F32)] * 2
    q3 = qkvu.reshape(bsz, tp, QKVU_COLS)
    res = pl.pallas_call(
        _qk_prep_kernel,
        grid=(bsz, tp // tm),
        in_specs=[blk(MIX, 0), blk(MIX, 1), blk(MIX, 2), blk(LANE, 4 * MIX // LANE),
                  small(MIX), small(MIX), small(LANE), pl.BlockSpec((MIX, MIX), lambda b, i: (0, 0))],
        out_specs=out_specs,
        out_shape=out_shape,
        compiler_params=_params(("parallel", "parallel")),
    )(q3, q3, q3, q3, qg, kg, bf, ones_bd)
    qn, kn, lf, lfp = (a.reshape(bsz * tp, a.shape[-1]) for a in res[:4])
    return qn, kn, lf, lfp, (res[4], res[5]) if emit_kv else None


def _fox_attn_kernel(q_ref, k_ref, v_ref, lf_ref, o_ref, c_s):
    tp = q_ref.shape[0]
    cb = LANE
    row = lax.broadcasted_iota(jnp.int32, (cb, cb), 0)
    col = lax.broadcasted_iota(jnp.int32, (cb, cb), 1)
    tri = (row >= col).astype(F32)
    carry = jnp.zeros((1, LANE), F32)
    for blk in range(tp // cb):
        cblk = _dot_exact_lhs(tri, lf_ref[blk * cb:(blk + 1) * cb, :]) + carry
        c_s[blk * cb:(blk + 1) * cb, :] = cblk
        carry = cblk[cb - 1:cb, :]
    c = c_s[...]
    c_t = c.T
    q = q_ref[...]
    k = k_ref[...].astype(BF16)
    v = v_ref[...].astype(BF16)
    lane = lax.broadcasted_iota(jnp.int32, (1, LANE), 1)
    for r0 in range(0, tp, ATT_Q_TILE):
        r1 = min(r0 + ATT_Q_TILE, tp)
        tq = r1 - r0
        qi = lax.broadcasted_iota(jnp.int32, (tq, r1), 0) + r0
        ki = lax.broadcasted_iota(jnp.int32, (tq, r1), 1)
        causal = ki <= qi
        outs = []
        for e in range(2):
            head = (lane // HEAD_DIM) == e
            qe = jnp.where(head, q[r0:r1, :], jnp.zeros((), BF16))
            s = _dot(qe, k[:r1, :], NT)
            s = s + c[r0:r1, e:e + 1] - c_t[e:e + 1, :r1]
            s = jnp.where(causal, s, -1e30)
            m = jnp.max(s, axis=-1, keepdims=True)
            p = jnp.exp(s - m)
            l = jnp.sum(p, axis=-1, keepdims=True)
            outs.append(_dot(p.astype(BF16), v[:r1, :]) / l)
        o_ref[r0:r1, :] = jnp.where((lane // HEAD_DIM) == 0, outs[0], outs[1]).astype(o_ref.dtype)


def _fox_attention(qn, kn, qkvu, lfp, bsz, tp):
    hp = N_HEADS // 2
    out = pl.pallas_call(
        _fox_attn_kernel,
        grid=(bsz, hp),
        in_specs=[
            pl.BlockSpec((None, tp, LANE), lambda b, h: (b, 0, h)),
            pl.BlockSpec((None, tp, LANE), lambda b, h: (b, 0, h)),
            pl.BlockSpec((None, tp, LANE), lambda b, h: (b, 0, 2 * MIX // LANE + h)),
            pl.BlockSpec((None, tp, LANE), lambda b, h: (b, 0, h)),
        ],
        out_specs=pl.BlockSpec((None, tp, LANE), lambda b, h: (b, 0, h)),
        out_shape=jax.ShapeDtypeStruct((bsz, tp, MIX), BF16),
        scratch_shapes=[pltpu.VMEM((tp, LANE), F32)],
        compiler_params=_params(("parallel", "parallel")),
    )(qn.reshape(bsz, tp, MIX), kn.reshape(bsz, tp, MIX), qkvu.reshape(bsz, tp, QKVU_COLS),
      lfp.reshape(bsz, tp, MIX))
    return out.reshape(bsz * tp, MIX)


def _fox_decode_kernel(pt_ref, q_ref, kn_ref, vn_ref, lfn_ref, *rest, pages_per_step):
    del pt_ref
    g = pages_per_step
    k_refs, v_refs, lf_refs = rest[:g], rest[g:2 * g], rest[2 * g:3 * g]
    o_ref, m_s, l_s, acc_s, carry_s = rest[3 * g:]
    p = pl.program_id(1)
    hrow = lax.broadcasted_iota(jnp.int32, (N_HEADS, MIX), 0)
    hlane = lax.broadcasted_iota(jnp.int32, (N_HEADS, MIX), 1) // HEAD_DIM
    headmask = hrow == hlane
    qb = jnp.where(headmask, q_ref[...], 0.0)

    @pl.when(p == 0)
    def _():
        m_s[...] = jnp.sum(qb * kn_ref[...], axis=1, keepdims=True)
        l_s[...] = jnp.ones_like(l_s)
        acc_s[...] = jnp.where(headmask, vn_ref[...], 0.0)
        carry_s[...] = lfn_ref[...]

    srow = lax.broadcasted_iota(jnp.int32, (PAGE, PAGE), 0)
    scol = lax.broadcasted_iota(jnp.int32, (PAGE, PAGE), 1)
    later = (srow > scol).astype(F32)
    qbb = qb.astype(BF16)
    carry = carry_s[...]
    logits = []
    for i in range(g):
        lf = lf_refs[i][...]
        kt = k_refs[i][...].reshape(MIX, PAGE).astype(BF16)
        logits.append(_dot(qbb, kt) + (carry + _dot_exact_rhs(lf, later)))
        carry = carry + jnp.sum(lf, axis=1, keepdims=True)
    carry_s[...] = carry
    m_old = m_s[...]
    m_new = m_old
    for s in logits:
        m_new = jnp.maximum(m_new, jnp.max(s, axis=1, keepdims=True))
    alpha = jnp.exp(m_old - m_new)
    l = l_s[...] * alpha
    acc = acc_s[...] * alpha
    for i in range(g):
        pr = jnp.exp(logits[i] - m_new)
        l = l + jnp.sum(pr, axis=1, keepdims=True)
        acc = acc + _dot(pr.astype(BF16), v_refs[i][...].reshape(MIX, PAGE).astype(BF16), NT)
    m_s[...], l_s[...], acc_s[...] = m_new, l, acc

    @pl.when(p == pl.num_programs(1) - 1)
    def _():
        o_ref[...] = jnp.sum(jnp.where(headmask, acc / l, 0.0), axis=0, keepdims=True)


def _fox_decode(layer, q, kn, vn, lfn, cache_kt, cache_vt, cache_lft, page_table):
    bsz, n_pages = page_table.shape
    g = DECODE_PAGES_PER_STEP
    assert n_pages % g == 0

    def page_spec(block, i):
        return pl.BlockSpec((None, None) + block,
                            lambda b, p, pt: (layer, pt[b, n_pages - 1 - (p * g + i)]) + (0,) * len(block))

    tok = lambda r, w: pl.BlockSpec((None, r, w), lambda b, p, pt: (b, 0, 0))
    in_specs = [tok(1, MIX), tok(1, MIX), tok(1, MIX), tok(N_HEADS, 1)]
    in_specs += [page_spec((N_HEADS, HEAD_DIM, PAGE), i) for i in range(g)]
    in_specs += [page_spec((N_HEADS, HEAD_DIM, PAGE), i) for i in range(g)]
    in_specs += [page_spec((N_HEADS, PAGE), i) for i in range(g)]
    grid_spec = pltpu.PrefetchScalarGridSpec(
        num_scalar_prefetch=1,
        grid=(bsz, n_pages // g),
        in_specs=in_specs,
        out_specs=tok(1, MIX),
        scratch_shapes=[pltpu.VMEM((N_HEADS, 1), F32), pltpu.VMEM((N_HEADS, 1), F32),
                        pltpu.VMEM((N_HEADS, MIX), F32), pltpu.VMEM((N_HEADS, 1), F32)],
    )
    return pl.pallas_call(
        functools.partial(_fox_decode_kernel, pages_per_step=g),
        grid_spec=grid_spec,
        out_shape=jax.ShapeDtypeStruct((bsz, 1, MIX), F32),
        compiler_params=_params(("parallel", "arbitrary")),
    )(page_table, q, kn, vn, lfn, *([cache_kt] * g), *([cache_vt] * g), *([cache_lft] * g))


def _rwkv_prep_kernel(rw_ref, prev_ref, sh0_ref, mu_ref, w0_ref, w2_ref, a0_ref, a2_ref, g2_ref,
                      kk_ref, ka_ref, rk_ref, ones_ref,
                      r_out, lw_out, k_out, v_out, kk_out, b_out, g_out, bonus_out):
    i = pl.program_id(1)
    rw = rw_ref[...]
    tm = rw.shape[0]
    first = jnp.where(i == 0, sh0_ref[...], prev_ref[SUBLANE - 1:SUBLANE, :])
    trow = lax.broadcasted_iota(jnp.int32, (tm, 1), 0)
    prev = jnp.where(trow == 0, first, pltpu.roll(rw, 1, 0))
    z = rw + (prev - rw) * mu_ref[...]
    r = z[:, 0:MIX]
    k = z[:, MIX:2 * MIX]
    v = z[:, 2 * MIX:3 * MIX]
    xwa = z[:, 3 * MIX:3 * MIX + 2 * LORA_WA]
    xg = z[:, 3 * MIX + 2 * LORA_WA:]
    ones_bd = ones_ref[...]
    w = -_softplus(-(w0_ref[...] + _dot(jnp.tanh(xwa).astype(BF16), w2_ref[...]))) - 0.5
    lw_out[...] = -jnp.exp(w)
    a = jax.nn.sigmoid(a0_ref[...] + _dot(xwa.astype(BF16), a2_ref[...]))
    g_out[...] = _dot(jax.nn.sigmoid(xg).astype(BF16), g2_ref[...])
    kk = k * kk_ref[...]
    kk = kk / jnp.maximum(jnp.sqrt(_segsum(kk * kk, ones_bd)), 1e-12)
    k2 = k * (1.0 + (a - 1.0) * ka_ref[...])
    r_out[...] = r.astype(BF16)
    k_out[...] = k2.astype(BF16)
    v_out[...] = v.astype(BF16)
    kk_out[...] = kk.astype(BF16)
    b_out[...] = (kk * a).astype(BF16)
    bonus_out[...] = _segsum(r * k2 * rk_ref[...], ones_bd) * v


def _rwkv_prep(rwp, shift0, prm, bsz, tp):
    tm = _tile(tp, 544)
    nb = tm // SUBLANE
    small = lambda a: pl.BlockSpec(a.shape, lambda b, i: (0, 0))
    consts = [prm["mu"], prm["w0"], prm["w2"], prm["a0"], prm["a2"], prm["g2"], prm["k_k"], prm["k_a"],
              prm["r_k"], prm["ones_bd"]]
    rw3 = rwp.reshape(bsz, tp, RW_PAD)
    outs = pl.pallas_call(
        _rwkv_prep_kernel,
        grid=(bsz, tp // tm),
        in_specs=[
            pl.BlockSpec((None, tm, RW_PAD), lambda b, i: (b, i, 0)),
            pl.BlockSpec((None, SUBLANE, RW_PAD), lambda b, i: (b, jnp.maximum(i * nb - 1, 0), 0)),
            pl.BlockSpec((None, 1, RW_PAD), lambda b, i: (b, 0, 0)),
        ] + [small(c) for c in consts],
        out_specs=[pl.BlockSpec((None, tm, MIX), lambda b, i: (b, i, 0))] * 8,
        out_shape=[jax.ShapeDtypeStruct((bsz, tp, MIX), dt) for dt in (BF16, F32, BF16, BF16, BF16, BF16, F32, F32)],
        compiler_params=_params(("parallel", "parallel")),
    )(rw3, rw3, shift0, *consts)
    return outs


def _rwkv_scan_kernel(r_ref, lw_ref, k_ref, v_ref, kk_ref, b_ref, s0_ref, o_ref, s_out, s_s, *, chunk, t_valid):
    c = pl.program_id(1)
    n = chunk

    @pl.when(c == 0)
    def _():
        s_s[...] = s0_ref[...]

    nseq = r_ref.shape[0]
    trow = lax.broadcasted_iota(jnp.int32, (n, 1), 0) + c * n
    valid = trow < t_valid
    row = lax.broadcasted_iota(jnp.int32, (n, n), 0)
    col = lax.broadcasted_iota(jnp.int32, (n, n), 1)
    incl = (row >= col).astype(F32)
    eye = (row == col).astype(F32)
    row2 = lax.broadcasted_iota(jnp.int32, (n, 2 * n), 0)
    col2 = lax.broadcasted_iota(jnp.int32, (n, 2 * n), 1)
    col2 = jnp.where(col2 >= n, col2 - n, col2)
    strict2 = row2 > col2
    incl2 = row2 >= col2

    a_m, b_m, bk, v_h, wl_h, s_all = [], [], [], [], [], []
    for i in range(nseq):
        lw = jnp.where(valid, lw_ref[i], 0.0)
        kk = jnp.where(valid, kk_ref[i].astype(F32), 0.0)
        bb = jnp.where(valid, b_ref[i].astype(F32), 0.0)
        kx = jnp.where(valid, k_ref[i].astype(F32), 0.0)
        vx = jnp.where(valid, v_ref[i].astype(F32), 0.0)
        rx = r_ref[i].astype(F32)
        ci = _dot_exact_lhs(incl, lw)
        kkd = kk * jnp.exp(ci - lw)
        rd = rx * jnp.exp(ci)
        einv = jnp.exp(-ci)
        bd = bb * einv
        kd = kx * einv
        cl = ci[n - 1:n, :]
        wl = jnp.exp(cl)
        dl = jnp.exp(cl - ci)
        bdw = bb * dl
        kdw = kx * dl
        for h in range(N_HEADS):
            sl = slice(h * HEAD_DIM, (h + 1) * HEAD_DIM)
            a_m.append(jnp.concatenate([kkd[:, sl], rd[:, sl]], axis=0))
            b_m.append(jnp.concatenate([bd[:, sl], kd[:, sl]], axis=0))
            bk.append(jnp.concatenate([bdw[:, sl], kdw[:, sl]], axis=0))
            v_h.append(vx[:, sl])
            wl_h.append(wl[:, sl])
            s_all.append(s_s[i, h])
    ch = range(nseq * N_HEADS)
    pm = [_dot1(a_m[j], b_m[j], NT) for j in ch]
    p_top = [jnp.where(strict2, pm[j][:n, :], 0.0) for j in ch]
    p_bot = [jnp.where(incl2, pm[j][n:, :], 0.0) for j in ch]
    gm = [_dot1(a_m[j], s_all[j], NT) for j in ch]
    x = [-p_top[j][:, :n] for j in ch]
    tinv = [eye + x[j] for j in ch]
    span = 2
    while span < n:
        x = [_dot1(x[j], x[j]) for j in ch]
        tinv = [tinv[j] + _dot1(tinv[j], x[j]) for j in ch]
        span *= 2
    rhs = [gm[j][:n, :] + _dot1(p_top[j][:, n:], v_h[j]) for j in ch]
    u = [-_dot1(tinv[j], rhs[j]) for j in ch]
    uv = [jnp.concatenate([u[j], v_h[j]], axis=0) for j in ch]
    outs = [gm[j][n:, :] + _dot1(p_bot[j], uv[j]) for j in ch]
    s_new = [s_all[j] * wl_h[j] + _dot1(uv[j], bk[j], TN) for j in ch]
    for i in range(nseq):
        for h in range(N_HEADS):
            s_s[i, h] = s_new[i * N_HEADS + h]
        o_ref[i] = jnp.concatenate(outs[i * N_HEADS:(i + 1) * N_HEADS], axis=1)

    @pl.when(c == pl.num_programs(1) - 1)
    def _():
        s_out[...] = s_s[...]


def _rwkv_scan(r, lw, k, v, kk, b, s0, bsz, tp, t_valid, chunk):
    ns = RWKV_SEQS_PER_STEP
    assert bsz % ns == 0
    blk = pl.BlockSpec((ns, chunk, MIX), lambda bi, c: (bi, c, 0))
    st = pl.BlockSpec((ns, N_HEADS, HEAD_DIM, HEAD_DIM), lambda bi, c: (bi, 0, 0, 0))
    o, s_out = pl.pallas_call(
        functools.partial(_rwkv_scan_kernel, chunk=chunk, t_valid=t_valid),
        grid=(bsz // ns, tp // chunk),
        in_specs=[blk] * 6 + [st],
        out_specs=[blk, st],
        out_shape=[jax.ShapeDtypeStruct((bsz, tp, MIX), F32),
                   jax.ShapeDtypeStruct((bsz, N_HEADS, HEAD_DIM, HEAD_DIM), F32)],
        scratch_shapes=[pltpu.VMEM((ns, N_HEADS, HEAD_DIM, HEAD_DIM), F32)],
        compiler_params=_params(("parallel", "arbitrary")),
    )(r, lw, k, v, kk, b, s0)
    return o.reshape(bsz * tp, MIX), s_out


def _s5_kernel(u_ref, x0_ref, a_ref, bm_ref, cm_ref, d_ref, gw_ref, gb_ref, o_ref, xo_ref, buf_s, st_s,
               *, t_valid):
    c = pl.program_id(0)
    bsz, tc, _ = u_ref.shape
    tcp = buf_s.shape[1] // bsz
    half = S5_LANES // 2
    hc = MIX // 2
    nblk = S5_LANES // LANE
    hblk = half // LANE

    @pl.when(c == 0)
    def _():
        st_s[...] = x0_ref[...]

    u2 = u_ref[...].reshape(bsz * tc, MIX)
    ub = u2.astype(BF16)
    for part in range(2):
        for j in range(2):
            col0 = part * S5_LANES + j * half
            res = _dot(ub[:, j * hc:(j + 1) * hc], bm_ref[j * hc:(j + 1) * hc, col0:col0 + half])
            for k in range(hblk):
                for b in range(bsz):
                    buf_s[col0 // LANE + k, b * tcp:b * tcp + tc, :] = res[b * tc:(b + 1) * tc,
                                                                           k * LANE:(k + 1) * LANE]

    gb = max(1, 2 * nblk // bsz)
    for g0 in range(0, nblk, gb):
        blks = list(range(g0, g0 + gb))
        lane = lambda k: slice(k * LANE, (k + 1) * LANE)
        ar = [jnp.broadcast_to(a_ref[:, lane(k)], (bsz, LANE)) for k in blks]
        ai = [jnp.broadcast_to(a_ref[:, lane(nblk + k)], (bsz, LANE)) for k in blks]

        def body(t, carry, blks=blks, ar=ar, ai=ai):
            rows = pl.ds(t, bsz, stride=tcp)
            out = []
            for n, k in enumerate(blks):
                xr, xi = carry[n]
                nxr = ar[n] * xr - ai[n] * xi + buf_s[k, rows, :]
                nxi = ar[n] * xi + ai[n] * xr + buf_s[nblk + k, rows, :]
                buf_s[k, rows, :] = nxr
                buf_s[nblk + k, rows, :] = nxi
                out.append((nxr, nxi))
            return tuple(out)

        init = tuple((st_s[:, lane(k)], st_s[:, lane(nblk + k)]) for k in blks)
        fin = lax.fori_loop(0, tc, body, init)
        for n, k in enumerate(blks):
            st_s[:, lane(k)] = fin[n][0]
            st_s[:, lane(nblk + k)] = fin[n][1]

    c_valid, t_in = divmod(t_valid - 1, tc)

    @pl.when(c == c_valid)
    def _():
        for k in range(2 * nblk):
            xo_ref[:, k * LANE:(k + 1) * LANE] = buf_s[k, pl.ds(t_in, bsz, stride=tcp), :]

    ys = []
    for j in range(2):
        acc = None
        for part in range(2):
            col0 = part * S5_LANES + j * half
            xs = jnp.concatenate(
                [jnp.concatenate([buf_s[col0 // LANE + k, b * tcp:b * tcp + tc, :] for b in range(bsz)], axis=0)
                 for k in range(hblk)], axis=1)
            d = _dot(xs.astype(BF16), cm_ref[col0:col0 + half, j * hc:(j + 1) * hc])
            acc = d if acc is None else acc + d
        ys.append(acc)
    y = jnp.concatenate(ys, axis=1) + d_ref[...] * u2
    y = 0.5 * y * (1.0 + jnp.tanh(0.7978845608028654 * (y + 0.044715 * (y * y * y))))
    gate = jax.nn.sigmoid(_dot(y.astype(BF16), gw_ref[...]) + gb_ref[...])
    o_ref[...] = (y * gate).reshape(bsz, tc, MIX).astype(o_ref.dtype)


def _s5(qkvu, x0, prm, bsz, tp, t_valid, tc):
    n2 = 2 * S5_LANES
    full = lambda a: pl.BlockSpec(a.shape, lambda c: (0,) * a.ndim)
    consts = [prm["s5_a"], prm["s5_bm"], prm["s5_cm"], prm["s5_d"], prm["s5_gw"], prm["s5_gb"]]
    o, xo = pl.pallas_call(
        functools.partial(_s5_kernel, t_valid=t_valid),
        grid=(tp // tc,),
        in_specs=[pl.BlockSpec((bsz, tc, MIX), lambda c: (0, c, 3)), full(x0)] + [full(a) for a in consts],
        out_specs=[pl.BlockSpec((bsz, tc, MIX), lambda c: (0, c, 0)), pl.BlockSpec((bsz, n2), lambda c: (0, 0))],
        out_shape=[jax.ShapeDtypeStruct((bsz, tp, MIX), BF16), jax.ShapeDtypeStruct((bsz, n2), F32)],
        scratch_shapes=[pltpu.VMEM((n2 // LANE, bsz * (tc + S5_SLAB_PAD), LANE), F32),
                        pltpu.VMEM((bsz, n2), F32)],
        compiler_params=_params(("arbitrary",)),
    )(qkvu.reshape(bsz, tp, QKVU_COLS), x0, *consts)
    return o.reshape(bsz * tp, MIX), xo


def _merge_kernel(x_ref, hn_ref, oa_ref, ob_ref, g_ref, bonus_ref, oc_ref, lw_ref, lb_ref, ones_ref,
                  wg_ref, wb_ref, wo_ref, o_ref):
    ones_bd = ones_ref[...]
    o = ob_ref[...]
    mu = _segsum(o, ones_bd) * (1.0 / HEAD_DIM)
    dlt = o - mu
    var = _segsum(dlt * dlt, ones_bd) * (1.0 / HEAD_DIM)
    ob = (dlt * lax.rsqrt(var + LNX_EPS) * lw_ref[...] + lb_ref[...] + bonus_ref[...]) * g_ref[...]
    hn = hn_ref[...]
    branches = (oa_ref[...].astype(BF16), ob.astype(BF16), oc_ref[...].astype(BF16))
    merged = None
    for i, br in enumerate(branches):
        gate = jax.nn.sigmoid(_dot(hn, wg_ref[:, i * D_MODEL:(i + 1) * D_MODEL]))
        term = gate * _dot(br, wb_ref[i * MIX:(i + 1) * MIX, :])
        merged = term if merged is None else merged + term
    o_ref[...] = x_ref[...] + _dot(merged.astype(BF16), wo_ref[...])


def _merge(x, hn, oa, ob, g, bonus, oc, prm):
    rows = x.shape[0]
    tm = _tile(rows, 512)
    row = lambda w: pl.BlockSpec((tm, w), lambda i: (i, 0))
    full = lambda a: pl.BlockSpec(a.shape, lambda i: (0, 0))
    consts = [prm["lnx_w"], prm["lnx_b"], prm["ones_bd"], prm["w_gates"], prm["w_branch"], prm["w_out"]]
    return pl.pallas_call(
        _merge_kernel,
        grid=(rows // tm,),
        in_specs=[row(D_MODEL), row(D_MODEL), row(MIX), row(MIX), row(MIX), row(MIX), row(MIX)]
        + [full(c) for c in consts],
        out_specs=row(D_MODEL),
        out_shape=jax.ShapeDtypeStruct((rows, D_MODEL), F32),
        compiler_params=_params(("parallel",)),
    )(x, hn, oa, ob, g, bonus, oc, *consts)


def _layer_params(l, p):
    bf = lambda a: a.astype(BF16)
    row = lambda a, w=None: (a if w is None else jnp.pad(a, (0, w - a.shape[0]))).reshape(1, -1).astype(F32)
    w_in = p["w_in"][l]
    o_f = 3 * MIX
    o_rw = o_f + N_HEADS
    o_u = o_rw + RW_COLS
    o_g = o_u + MIX
    w_qkvu = jnp.concatenate(
        [w_in[:, :o_f], w_in[:, o_u:o_g], jnp.pad(w_in[:, o_f:o_rw], ((0, 0), (0, LANE - N_HEADS)))], axis=1)
    w_rw = jnp.pad(w_in[:, o_rw:o_u], ((0, 0), (0, RW_PAD - RW_COLS)))
    w_gates = w_in[:, o_g:]

    ar = p["s5_a_re"][l].astype(F32)
    ai = p["s5_a_im"][l].astype(F32)
    dt = jnp.exp(p["s5_log_dt"][l].astype(F32))[:, None]
    mag = jnp.exp(dt * ar)
    abar_re = mag * jnp.cos(dt * ai)
    abar_im = mag * jnp.sin(dt * ai)
    den = ar * ar + ai * ai
    nr, ni = abar_re - 1.0, abar_im
    z_re = (nr * ar + ni * ai) / den
    z_im = (ni * ar - nr * ai) / den
    b_re = p["s5_b_re"][l].astype(F32)
    b_im = p["s5_b_im"][l].astype(F32)
    bb_re = z_re[..., None] * b_re - z_im[..., None] * b_im
    bb_im = z_re[..., None] * b_im + z_im[..., None] * b_re
    eye = jnp.eye(S5_GROUPS, dtype=F32)
    blk_in = lambda m: jnp.einsum("gpc,gh->gchp", m, eye).reshape(MIX, S5_LANES)
    blk_out = lambda m: jnp.einsum("gcp,gh->gphc", m, eye).reshape(S5_LANES, MIX)
    s5_bm = jnp.concatenate([blk_in(bb_re), blk_in(bb_im)], axis=1)
    s5_cm = jnp.concatenate([blk_out(p["s5_c_re"][l].astype(F32)), -blk_out(p["s5_c_im"][l].astype(F32))], axis=0)

    hd = lax.broadcasted_iota(jnp.int32, (MIX, MIX), 0) // HEAD_DIM
    hd_t = lax.broadcasted_iota(jnp.int32, (MIX, MIX), 1) // HEAD_DIM
    return dict(
        ffn1_norm=row(p["ffn1_norm"][l]), ffn1_w_in=bf(p["ffn1_w_in"][l]), ffn1_w_out=bf(p["ffn1_w_out"][l]),
        ffn2_norm=row(p["ffn2_norm"][l]), ffn2_w_in=bf(p["ffn2_w_in"][l]), ffn2_w_out=bf(p["ffn2_w_out"][l]),
        mix_norm=row(p["mix_norm"][l]),
        w_qkvu=bf(w_qkvu), w_rw=bf(w_rw), w_gates=bf(w_gates),
        q_gain=row(jnp.tile(p["fox_q_gain"][l], N_HEADS)), k_gain=row(jnp.tile(p["fox_k_gain"][l], N_HEADS)),
        b_f=row(p["fox_b_f"][l], LANE),
        ones_bd=(hd == hd_t).astype(BF16),
        mu=row(p["rwkv_mu"][l], RW_PAD), w0=row(p["rwkv_w0"][l]), a0=row(p["rwkv_a0"][l]),
        w2=bf(jnp.pad(p["rwkv_w2"][l], ((0, LORA_WA), (0, 0)))),
        a2=bf(jnp.pad(p["rwkv_a2"][l], ((LORA_WA, 0), (0, 0)))),
        g2=bf(jnp.pad(p["rwkv_g2"][l], ((0, RW_PAD - RW_COLS), (0, 0)))),
        k_k=row(p["rwkv_k_k"][l]), k_a=row(p["rwkv_k_a"][l]), r_k=row(p["rwkv_r_k"][l].reshape(-1)),
        lnx_w=row(p["rwkv_lnx_w"][l]), lnx_b=row(p["rwkv_lnx_b"][l]),
        s5_a=jnp.concatenate([abar_re.reshape(1, -1), abar_im.reshape(1, -1)], axis=1),
        s5_bm=bf(s5_bm), s5_cm=bf(s5_cm), s5_d=row(p["s5_d"][l].reshape(-1)),
        s5_gw=bf(p["s5_glu_w"][l]), s5_gb=row(p["s5_glu_b"][l]),
        w_branch=bf(p["w_branch"][l]), w_out=bf(p["w_out"][l]),
    )


def _mixer_inputs(x, prm, bsz, tp, t_valid):
    x, hn = _ffn(x, prm["ffn1_norm"], prm["ffn1_w_in"], prm["ffn1_w_out"], prm["mix_norm"], True)
    qkvu = _matmul(hn, prm["w_qkvu"], QKVU_COLS)
    rwp = _matmul(hn, prm["w_rw"], RW_PAD)
    qn, kn, lf, lfp, kv = _qk_prep(qkvu, prm["q_gain"], prm["k_gain"], prm["b_f"], prm["ones_bd"], bsz, tp, t_valid)
    return x, qkvu, rwp, hn, qn, kn, lf, lfp, kv


def _layer_tail(x, oa, qkvu, rwp, hn, prm, shift0, s0, x0, bsz, tp, t_valid, chunk, tc):
    r, lw, k2, v, kk, b, g, bonus = _rwkv_prep(rwp, shift0, prm, bsz, tp)
    ob, s_out = _rwkv_scan(r, lw, k2, v, kk, b, s0, bsz, tp, t_valid, chunk)
    oc, xo = _s5(qkvu, x0, prm, bsz, tp, t_valid, tc)
    flat = lambda a: a.reshape(bsz * tp, MIX)
    x = _merge(x, hn, oa, ob, flat(g), flat(bonus), oc, prm)
    x, _ = _ffn(x, prm["ffn2_norm"], prm["ffn2_w_in"], prm["ffn2_w_out"], prm["ffn2_norm"], False)
    return x, s_out, xo


def kernel(x_prompt, x_sample, cache_k, cache_v, cache_logf, page_table, state_rwkv, state_shift, state_s5_re, state_s5_im, meta_tokens, ffn1_norm, ffn1_w_in, ffn1_w_out, mix_norm, w_in, fox_b_f, fox_q_gain, fox_k_gain, rwkv_mu, rwkv_w0, rwkv_w2, rwkv_a0, rwkv_a2, rwkv_g2, rwkv_k_k, rwkv_k_a, rwkv_r_k, rwkv_lnx_w, rwkv_lnx_b, s5_a_re, s5_a_im, s5_log_dt, s5_b_re, s5_b_im, s5_c_re, s5_c_im, s5_d, s5_glu_w, s5_glu_b, w_branch, w_out, ffn2_norm, ffn2_w_in, ffn2_w_out):
    p = dict(ffn1_norm=ffn1_norm, ffn1_w_in=ffn1_w_in, ffn1_w_out=ffn1_w_out, mix_norm=mix_norm,
             w_in=w_in, fox_b_f=fox_b_f, fox_q_gain=fox_q_gain, fox_k_gain=fox_k_gain,
             rwkv_mu=rwkv_mu, rwkv_w0=rwkv_w0, rwkv_w2=rwkv_w2, rwkv_a0=rwkv_a0, rwkv_a2=rwkv_a2,
             rwkv_g2=rwkv_g2, rwkv_k_k=rwkv_k_k, rwkv_k_a=rwkv_k_a, rwkv_r_k=rwkv_r_k,
             rwkv_lnx_w=rwkv_lnx_w, rwkv_lnx_b=rwkv_lnx_b, s5_a_re=s5_a_re, s5_a_im=s5_a_im,
             s5_log_dt=s5_log_dt, s5_b_re=s5_b_re, s5_b_im=s5_b_im, s5_c_re=s5_c_re, s5_c_im=s5_c_im,
             s5_d=s5_d, s5_glu_w=s5_glu_w, s5_glu_b=s5_glu_b, w_branch=w_branch, w_out=w_out,
             ffn2_norm=ffn2_norm, ffn2_w_in=ffn2_w_in, ffn2_w_out=ffn2_w_out)
    depth = w_in.shape[0]
    prms = [_layer_params(l, p) for l in range(depth)]

    bp, seq, _ = x_prompt.shape
    t_p = seq + N_META
    tp = -(-t_p // SEQ_ALIGN) * SEQ_ALIGN
    meta = jnp.broadcast_to(meta_tokens.astype(F32)[None], (bp, N_META, D_MODEL))
    xp = jnp.concatenate([meta, x_prompt], axis=1)
    xp = jnp.pad(xp, ((0, 0), (0, tp - t_p), (0, 0))).reshape(bp * tp, D_MODEL)
    z_shift = jnp.zeros((bp, 1, RW_PAD), F32)
    z_s = jnp.zeros((bp, N_HEADS, HEAD_DIM, HEAD_DIM), F32)
    z_x = jnp.zeros((bp, 2 * S5_LANES), F32)
    seqs = lambda a, w: a.reshape(bp, tp, w)
    pk, pv, plf, prw, psh, pre, pim = [], [], [], [], [], [], []
    for l in range(depth):
        prm = prms[l]
        xp, qkvu, rwp, hn, qn, kn, lf, lfp, kv = _mixer_inputs(xp, prm, bp, tp, t_p)
        oa = _fox_attention(qn, kn, qkvu, lfp, bp, tp)
        xp, s_out, xo = _layer_tail(xp, oa, qkvu, rwp, hn, prm, z_shift, z_s, z_x, bp, tp, t_p,
                                    RWKV_CHUNK, S5_CHUNK)
        if kv is None:
            kv = (seqs(kn, MIX)[:, :t_p], seqs(qkvu, QKVU_COLS)[:, :t_p, 2 * MIX:3 * MIX])
        pk.append(kv[0].reshape(bp, t_p, N_HEADS, HEAD_DIM))
        pv.append(kv[1].reshape(bp, t_p, N_HEADS, HEAD_DIM))
        plf.append(seqs(lf, LANE)[:, :t_p, :N_HEADS])
        prw.append(s_out)
        psh.append(seqs(rwp, RW_PAD)[:, t_p - 1, :RW_COLS])
        pre.append(xo[:, :S5_LANES].reshape(bp, S5_GROUPS, S5_STATE))
        pim.append(xo[:, S5_LANES:].reshape(bp, S5_GROUPS, S5_STATE))
    y_prompt = seqs(xp, D_MODEL)[:, N_META:t_p]

    bs = x_sample.shape[0]
    ts = SUBLANE
    xs = jnp.pad(x_sample, ((0, 0), (0, ts - 1), (0, 0))).reshape(bs * ts, D_MODEL)
    ck =cache_k.transpose(0, 1, 3, 4, 2)
    cv = cache_v.transpose(0, 1, 3, 4, 2)
    clf = cache_logf.transpose(0, 1, 3, 2)
    tok0 = lambda a, w: a.reshape(bs, ts, w)[:, 0]
    sk, sv, slf, srw, ssh, sre, sim = [], [], [], [], [], [], []
    for l in range(depth):
        prm = prms[l]
        xs, qkvu, rwp, hn, qn, kn, lf, lfp, _ = _mixer_inputs(xs, prm, bs, ts, 1)
        q_new = tok0(qn, MIX).astype(F32)
        k_new = tok0(kn, MIX)
        v_new = tok0(qkvu, QKVU_COLS)[:, 2 * MIX:3 * MIX]
        lf_new = tok0(lf, LANE)[:, :N_HEADS]
        oa = _fox_decode(l, q_new[:, None], k_new[:, None], v_new[:, None], lf_new[:, :, None], ck, cv, clf,
                         page_table)
        oa = jnp.pad(oa, ((0, 0), (0, ts - 1), (0, 0))).reshape(bs * ts, MIX)
        shift0 = jnp.pad(state_shift[l].astype(F32), ((0, 0), (0, RW_PAD - RW_COLS)))[:, None]
        x0 = jnp.concatenate([state_s5_re[l].reshape(bs, S5_LANES), state_s5_im[l].reshape(bs, S5_LANES)],
                             axis=1).astype(F32)
        xs, s_out, xo = _layer_tail(xs, oa, qkvu, rwp, hn, prm, shift0, state_rwkv[l].astype(F32), x0,
                                    bs, ts, 1, ts, ts)
        sk.append(k_new.reshape(bs, 1, N_HEADS, HEAD_DIM))
        sv.append(v_new.reshape(bs, 1, N_HEADS, HEAD_DIM))
        slf.append(lf_new.reshape(bs, 1, N_HEADS))
        srw.append(s_out)
        ssh.append(tok0(rwp, RW_PAD)[:, :RW_COLS])
        sre.append(xo[:, :S5_LANES].reshape(bs, S5_GROUPS, S5_STATE))
        sim.append(xo[:, S5_LANES:].reshape(bs, S5_GROUPS, S5_STATE))
    y_sample = tok0(xs, D_MODEL).reshape(bs, 1, D_MODEL)

    st = lambda xs_: jnp.stack(xs_, axis=0)
    return (y_prompt, y_sample, st(pk), st(pv), st(plf), st(prw), st(psh), st(pre), st(pim),
            st(sk), st(sv), st(slf), st(srw), st(ssh), st(sre), st(sim))
```
